```python
import jax, jax.numpy as jnp
from jax import lax
import numpy as np

D_MODEL = 1024
BATCH = 2
SEQ = 8192
DEPTH = 1
DEC_BATCH = 32
DEC_SEQ = 1
PAST_LEN = 16384
PAGE_SIZE = 128

HEAD_DIM = 128
ROT_DIM = HEAD_DIM // 4
ROPE_THETA = 500000.0
DIL_GROUPS = ((128, 1), (512, 4), (2048, 16))
HEADS_PER_GROUP = 4
N_Q_A = HEADS_PER_GROUP * len(DIL_GROUPS)
N_KV_A = HEADS_PER_GROUP
WIN_MAX = 2048
BAND_BLK = 128
CONV_CH = 512
CONV_W = 31
N_HEADS_M = 4
N_MEM = 256
N_BRANCH = 3
PEER_HEADS = 8
N_KEYS = 128
N_EXPERTS = N_KEYS * N_KEYS
PEER_TOPK = 16
PK_HALF = 128
PEER_BLK = 128
EPS = 1e-6

QA_W = N_Q_A * HEAD_DIM
KVA_W = N_KV_A * HEAD_DIM
CONV_IN_W = 2 * CONV_CH
QM_W = N_HEADS_M * HEAD_DIM
GATE_W = N_BRANCH * D_MODEL
IN_W = QA_W + 2 * KVA_W + CONV_IN_W + QM_W + GATE_W
SPLIT_IDX = (QA_W, QA_W + KVA_W, QA_W + 2 * KVA_W, QA_W + 2 * KVA_W + CONV_IN_W,
             QA_W + 2 * KVA_W + CONV_IN_W + QM_W)
A_OUT_W = N_KV_A * HEAD_DIM
M_OUT_W = N_HEADS_M * HEAD_DIM

kernel_name = 'hybrid_dilated_conformer_peer_step'


def rms_norm(x, g):
    xf = x.astype(jnp.float32)
    y = xf * lax.rsqrt(jnp.mean(xf * xf, axis=-1, keepdims=True) + EPS)
    return (y * g.astype(jnp.float32)).astype(x.dtype)


def rotary(x, pos):
    half = ROT_DIM // 2
    inv = jnp.float32(ROPE_THETA) ** (-jnp.arange(half, dtype=jnp.float32) / half)
    ang = pos.astype(jnp.float32)[:, None] * inv[None, :]
    cos = jnp.cos(ang)[:, None, :]
    sin = jnp.sin(ang)[:, None, :]
    xr = x[..., :ROT_DIM].astype(jnp.float32)
    x1, x2 = xr[..., :half], xr[..., half:]
    rot = jnp.concatenate([x1 * cos - x2 * sin, x2 * cos + x1 * sin], axis=-1).astype(x.dtype)
    return jnp.concatenate([rot, x[..., ROT_DIM:]], axis=-1)


def softmax_lse(s):
    m = jnp.max(s, axis=-1, keepdims=True)
    p = jnp.exp(s - m)
    den = jnp.sum(p, axis=-1, keepdims=True)
    return p / den, (m + jnp.log(den))[..., 0]


def front(x, pos, g_norm1, w_in, b_gate, qn_a, kn_a, qn_m):
    N, T, _ = x.shape
    h = rms_norm(x, g_norm1)
    p = h @ w_in
    qa, ka, va, u, qm, gl = jnp.split(p, SPLIT_IDX, axis=-1)
    qa = rotary(rms_norm(qa.reshape(N, T, N_Q_A, HEAD_DIM), qn_a), pos)
    ka = rotary(rms_norm(ka.reshape(N, T, N_KV_A, HEAD_DIM), kn_a), pos)
    va = va.reshape(N, T, N_KV_A, HEAD_DIM)
    qm = rms_norm(qm.reshape(N, T, N_HEADS_M, HEAD_DIM), qn_m)
    gates = jax.nn.sigmoid(gl + b_gate).reshape(N, T, N_BRANCH, D_MODEL)
    return qa, ka, va, u, qm, gates


def band_group(q, k, v, window, dil):
    B, S, H, Dh = q.shape
    L = S // dil
    Lp = -(-L // BAND_BLK) * BAND_BLK
    nb = Lp // BAND_BLK
    wsub = window // dil

    def to_blocks(t):
        t = t.reshape(B, L, dil, H, Dh).transpose(0, 2, 1, 3, 4)
        t = jnp.pad(t, ((0, 0), (0, 0), (0, Lp - L), (0, 0), (0, 0)))
        return t.reshape(B, dil, nb, BAND_BLK, H, Dh)

    def with_prev(t):
        prev = jnp.pad(t, ((0, 0), (0, 0), (1, 0), (0, 0), (0, 0), (0, 0)))[:, :, :-1]
        return jnp.concatenate([prev, t], axis=3)

    qb = to_blocks(q)
    kk = with_prev(to_blocks(k))
    vv = with_prev(to_blocks(v))
    s = jnp.einsum('brnqhd,brnkhd->brnhqk', qb, kk,
                   preferred_element_type=jnp.float32) * (HEAD_DIM ** -0.5)
    qi = jnp.arange(BAND_BLK)[:, None]
    ki = jnp.arange(2 * BAND_BLK)[None, :]
    delta = qi + BAND_BLK - ki
    band = (delta >= 0) & (delta <= wsub)
    nidx = jnp.arange(nb)[:, None, None]
    mask = band[None] & ((nidx > 0) | (ki >= BAND_BLK)[None])
    s = jnp.where(mask[None, None, :, None], s, -jnp.inf)
    p, lse = softmax_lse(s)
    o = jnp.einsum('brnhqk,brnkhd->brnhqd', p, vv.astype(jnp.float32))
    o = o.transpose(0, 1, 2, 4, 3, 5).reshape(B, dil, Lp, H, Dh)[:, :, :L]
    o = o.transpose(0, 2, 1, 3, 4).reshape(B, S, H, Dh)
    lse = lse.transpose(0, 1, 2, 4, 3).reshape(B, dil, Lp, H)[:, :, :L]
    lse = lse.transpose(0, 2, 1, 3).reshape(B, S, H)
    return o, lse


def gather_group(q, k_all, v_all, n_buf, window, dil):
    T = q.shape[1]
    offs = jnp.arange(window // dil + 1) * dil
    idx = n_buf + jnp.arange(T)[:, None] - offs[None, :]
    valid = idx >= 0
    idx = jnp.maximum(idx, 0)
    ks = k_all[:, idx]
    vs = v_all[:, idx]
    s = jnp.einsum('bthd,btkhd->bthk', q, ks,
                   preferred_element_type=jnp.float32) * (HEAD_DIM ** -0.5)
    s = jnp.where(valid[None, :, None, :], s, -jnp.inf)
    p, lse = softmax_lse(s)
    o = jnp.einsum('bthk,btkhd->bthd', p, vs.astype(jnp.float32))
    return o, lse


def combine_groups(outs, lses):
    o = jnp.stack(outs, axis=0)
    w = jax.nn.softmax(jnp.stack(lses, axis=0), axis=0)
    return jnp.sum(w[..., None] * o, axis=0)


def dilated_attn_prompt(qa, ka, va):
    outs, lses = [], []
    for g, (win, dil) in enumerate(DIL_GROUPS):
        qg = qa[:, :, g * HEADS_PER_GROUP:(g + 1) * HEADS_PER_GROUP]
        o, l = band_group(qg, ka, va, win, dil)
        outs.append(o)
        lses.append(l)
    return combine_groups(outs, lses)


def dilated_attn_sample(qa, ka, va, win_k, win_v):
    n_buf = win_k.shape[1]
    k_all = jnp.concatenate([win_k.astype(ka.dtype), ka], axis=1)
    v_all = jnp.concatenate([win_v.astype(va.dtype), va], axis=1)
    outs, lses = [], []
    for g, (win, dil) in enumerate(DIL_GROUPS):
        qg = qa[:, :, g * HEADS_PER_GROUP:(g + 1) * HEADS_PER_GROUP]
        o, l = gather_group(qg, k_all, v_all, n_buf, win, dil)
        outs.append(o)
        lses.append(l)
    return combine_groups(outs, lses)


def mem_kv(mem, g_mem, w_mem_kv, kn_m):
    N, Mn, _ = mem.shape
    kv = rms_norm(mem, g_mem) @ w_mem_kv
    mk = rms_norm(kv[..., :M_OUT_W].reshape(N, Mn, N_HEADS_M, HEAD_DIM), kn_m)
    mv = kv[..., M_OUT_W:].reshape(N, Mn, N_HEADS_M, HEAD_DIM)
    return mk, mv


def mem_attention(qm, mk, mv):
    s = jnp.einsum('bshd,bmhd->bhsm', qm, mk,
                   preferred_element_type=jnp.float32) * (HEAD_DIM ** -0.5)
    p = jax.nn.softmax(s, axis=-1)
    return jnp.einsum('bhsm,bmhd->bshd', p, mv.astype(jnp.float32))


def conformer_conv(u, left, w_dw, b_dw, ln_g, ln_b):
    glu = u[..., :CONV_CH] * jax.nn.sigmoid(u[..., CONV_CH:])
    seq = jnp.concatenate([left.astype(glu.dtype), glu], axis=1)
    y = lax.conv_general_dilated(seq, w_dw.astype(seq.dtype)[:, None, :], (1,), 'VALID',
                                 dimension_numbers=('NWC', 'WIO', 'NWC'),
                                 feature_group_count=CONV_CH)
    yf = y.astype(jnp.float32) + b_dw.astype(jnp.float32)
    mu = jnp.mean(yf, axis=-1, keepdims=True)
    var = jnp.mean(jnp.square(yf - mu), axis=-1, keepdims=True)
    yn = (yf - mu) * lax.rsqrt(var + EPS) * ln_g.astype(jnp.float32) + ln_b.astype(jnp.float32)
    out = (yn * jax.nn.sigmoid(yn)).astype(u.dtype)
    return out, seq[:, -(CONV_W - 1):]


def peer(x, w_pq, sub_keys, u_tab, v_tab):
    n_tok = x.shape[0]
    nblk = -(-n_tok // PEER_BLK)
    xp = jnp.pad(x, ((0, nblk * PEER_BLK - n_tok), (0, 0)))

    def block(xb):
        nb = xb.shape[0]
        q = (xb @ w_pq).reshape(nb, PEER_HEADS, 2, PK_HALF)
        s = jnp.einsum('bhcd,ckd->bhck', q, sub_keys, preferred_element_type=jnp.float32)
        s1, i1 = lax.top_k(s[:, :, 0], PEER_TOPK)
        s2, i2 = lax.top_k(s[:, :, 1], PEER_TOPK)
        cand = (s1[..., :, None] + s2[..., None, :]).reshape(nb, PEER_HEADS, PEER_TOPK * PEER_TOPK)
        sc, ci = lax.top_k(cand, PEER_TOPK)
        e = (jnp.take_along_axis(i1, ci // PEER_TOPK, axis=-1) * N_KEYS
             + jnp.take_along_axis(i2, ci % PEER_TOPK, axis=-1))
        g = jax.nn.softmax(sc, axis=-1)
        e = e.reshape(nb, PEER_HEADS * PEER_TOPK)
        g = g.reshape(nb, PEER_HEADS * PEER_TOPK)
        a = jnp.einsum('bd,bed->be', xb, u_tab[e], preferred_element_type=jnp.float32)
        hg = g * jax.nn.gelu(a)
        out = jnp.einsum('be,bed->bd', hg, v_tab[e].astype(jnp.float32))
        return out.astype(xb.dtype)

    ys = lax.map(block, xp.reshape(nblk, PEER_BLK, D_MODEL))
    return ys.reshape(nblk * PEER_BLK, D_MODEL)[:n_tok]


def back(x, a, b, m, gates, w_a_proj, w_b_proj, w_m_proj, w_o, g_norm2, w_pq, sub_keys, u_tab, v_tab):
    N, T, _ = x.shape
    dt = x.dtype
    ya = a.reshape(N, T, A_OUT_W).astype(dt) @ w_a_proj
    yb = b.astype(dt) @ w_b_proj
    ym = m.reshape(N, T, M_OUT_W).astype(dt) @ w_m_proj
    merged = gates[..., 0, :] * ya + gates[..., 1, :] * yb + gates[..., 2, :] * ym
    x1 = x + merged @ w_o
    h2 = rms_norm(x1, g_norm2).reshape(N * T, D_MODEL)
    return x1 + peer(h2, w_pq, sub_keys, u_tab, v_tab).reshape(N, T, D_MODEL)


def setup_inputs(seed: int = 0) -> dict:
    key = jax.random.key(seed)
    ks = iter(jax.random.split(key, 40))

    def nrm(shape, scale):
        return jax.random.normal(next(ks), shape, jnp.float32) * scale

    def gain(shape):
        return 1.0 + nrm(shape, 0.02)

    L = DEPTH
    D = D_MODEL
    win_s = min(WIN_MAX, PAST_LEN)
    return {
        'x_prompt': nrm((BATCH, SEQ, D), 1.0),
        'x_sample': nrm((DEC_BATCH, DEC_SEQ, D), 1.0),
        'mem_prompt': nrm((BATCH, N_MEM, D), 1.0),
        'cache_win_k': nrm((L, DEC_BATCH, win_s, N_KV_A, HEAD_DIM), 1.0),
        'cache_win_v': nrm((L, DEC_BATCH, win_s, N_KV_A, HEAD_DIM), 1.0),
        'cache_mem_k': nrm((L, DEC_BATCH, N_MEM, N_HEADS_M, HEAD_DIM), 1.0),
        'cache_mem_v': nrm((L, DEC_BATCH, N_MEM, N_HEADS_M, HEAD_DIM), 1.0),
        'state_conv': nrm((L, DEC_BATCH, CONV_W - 1, CONV_CH), 0.5),
        'g_norm1': gain((L, D)),
        'w_in': nrm((L, D, IN_W), D ** -0.5),
        'b_gate': nrm((L, GATE_W), 0.01),
        'qn_a': gain((L, HEAD_DIM)),
        'kn_a': gain((L, HEAD_DIM)),
        'qn_m': gain((L, HEAD_DIM)),
        'kn_m': gain((L, HEAD_DIM)),
        'g_mem': gain((L, D)),
        'w_mem_kv': nrm((L, D, 2 * M_OUT_W), D ** -0.5),
        'w_dw': nrm((L, CONV_W, CONV_CH), CONV_W ** -0.5),
        'b_dw': nrm((L, CONV_CH), 0.01),
        'ln_g': gain((L, CONV_CH)),
        'ln_b': nrm((L, CONV_CH), 0.01),
        'w_a_proj': nrm((L, A_OUT_W, D), A_OUT_W ** -0.5),
        'w_b_proj': nrm((L, CONV_CH, D), CONV_CH ** -0.5),
        'w_m_proj': nrm((L, M_OUT_W, D), M_OUT_W ** -0.5),
        'w_o': nrm((L, D, D), D ** -0.5),
        'g_norm2': gain((L, D)),
        'w_pq': nrm((L, D, PEER_HEADS * 2 * PK_HALF), D ** -0.5),
        'sub_keys': nrm((L, 2, N_KEYS, PK_HALF), PK_HALF ** -0.5),
        'u_tab': nrm((L, N_EXPERTS, D), D ** -0.5),
        'v_tab': nrm((L, N_EXPERTS, D), D ** -0.5),
    }


def reference(x_prompt, x_sample, mem_prompt, cache_win_k, cache_win_v, cache_mem_k, cache_mem_v,
              state_conv, g_norm1, w_in, b_gate, qn_a, kn_a, qn_m, kn_m, g_mem, w_mem_kv, w_dw, b_dw,
              ln_g, ln_b, w_a_proj, w_b_proj, w_m_proj, w_o, g_norm2, w_pq, sub_keys, u_tab, v_tab):
    S = x_prompt.shape[1]
    T = x_sample.shape[1]
    pos_p = jnp.arange(S)
    pos_s = PAST_LEN + jnp.arange(T)
    n_win_p = min(WIN_MAX, S)
    hp, hs = x_prompt, x_sample
    wk_p, wv_p, mk_p, mv_p, cv_p, wk_s, wv_s, cv_s = [], [], [], [], [], [], [], []
    for l in range(DEPTH):
        qa, ka, va, u, qm, gates = front(hp, pos_p, g_norm1[l], w_in[l], b_gate[l], qn_a[l], kn_a[l], qn_m[l])
        a = dilated_attn_prompt(qa, ka, va)
        mk, mv = mem_kv(mem_prompt, g_mem[l], w_mem_kv[l], kn_m[l])
        m = mem_attention(qm, mk, mv)
        left = jnp.zeros((hp.shape[0], CONV_W - 1, CONV_CH), u.dtype)
        b, conv_p = conformer_conv(u, left, w_dw[l], b_dw[l], ln_g[l], ln_b[l])
        hp = back(hp, a, b, m, gates, w_a_proj[l], w_b_proj[l], w_m_proj[l], w_o[l], g_norm2[l],
                  w_pq[l], sub_keys[l], u_tab[l], v_tab[l])
        wk_p.append(ka[:, S - n_win_p:])
        wv_p.append(va[:, S - n_win_p:])
        mk_p.append(mk)
        mv_p.append(mv)
        cv_p.append(conv_p)
        qa, ka, va, u, qm, gates = front(hs, pos_s, g_norm1[l], w_in[l], b_gate[l], qn_a[l], kn_a[l], qn_m[l])
        a = dilated_attn_sample(qa, ka, va, cache_win_k[l], cache_win_v[l])
        m = mem_attention(qm, cache_mem_k[l], cache_mem_v[l])
        b, conv_s = conformer_conv(u, state_conv[l], w_dw[l], b_dw[l], ln_g[l], ln_b[l])
        hs = back(hs, a, b, m, gates, w_a_proj[l], w_b_proj[l], w_m_proj[l], w_o[l], g_norm2[l],
                  w_pq[l], sub_keys[l], u_tab[l], v_tab[l])
        wk_s.append(ka)
        wv_s.append(va)
        cv_s.append(conv_s)
    return (hp, hs, jnp.stack(wk_p), jnp.stack(wv_p), jnp.stack(mk_p), jnp.stack(mv_p), jnp.stack(cv_p),
            jnp.stack(wk_s), jnp.stack(wv_s), jnp.stack(cv_s))
```

```python
import functools

import jax
import jax.numpy as jnp
import numpy as np
from jax import lax
from jax.experimental import pallas as pl
from jax.experimental.pallas import tpu as pltpu

D_MODEL = 1024
HEAD_DIM = 128
ROT_DIM = HEAD_DIM // 4
ROPE_THETA = 500000.0
DIL_GROUPS = ((128, 1), (512, 4), (2048, 16))
HEADS_PER_GROUP = 4
N_GROUPS = len(DIL_GROUPS)
BAND_BLK = 128
CONV_CH = 512
CONV_W = 31
N_HEADS_M = 4
N_MEM = 256
PEER_HEADS = 8
N_KEYS = 128
PEER_TOPK = 16
PAST_LEN = 16384
EPS = 1e-6

GROUP_W = HEADS_PER_GROUP * HEAD_DIM
QA_W = N_GROUPS * GROUP_W
IN_MAIN_W = QA_W + 2 * GROUP_W + 2 * CONV_CH + N_HEADS_M * HEAD_DIM
OFF_KA = QA_W
OFF_VA = OFF_KA + GROUP_W
OFF_U = OFF_VA + GROUP_W
OFF_QM = OFF_U + 2 * CONV_CH
SCALE = HEAD_DIM ** -0.5
NEG = -1e30
CONV_HALO = 32

VMEM_LIMIT = 56 * 1024 * 1024
BF16 = jnp.bfloat16
F32 = jnp.float32

NT_DIMS = (((1,), (1,)), ((), ()))


def _cparams(sem):
    return pltpu.CompilerParams(dimension_semantics=sem, vmem_limit_bytes=VMEM_LIMIT)


def _rms(x, g):
    return x * lax.rsqrt(jnp.mean(x * x, axis=-1, keepdims=True) + EPS) * g


def _sigmoid(x):
    return 1.0 / (1.0 + jnp.exp(-x))


def _dot(a, b):
    return jnp.dot(a, b, preferred_element_type=F32)


def _dot_nt(a, b):
    return lax.dot_general(a, b, NT_DIMS, preferred_element_type=F32)


def _front_kernel(x_ref, g1_ref, w_ref, qna_ref, kna_ref, qnm_ref, rc_ref, rsa_ref, rsb_ref,
                  qa0_ref, qa1_ref, qa2_ref, kab_ref, kaf_ref, vab_ref, vaf_ref, glu_ref, qm_ref):
    x = x_ref[0]
    hb = _rms(x, g1_ref[...]).astype(BF16)
    rc, rsa, rsb = rc_ref[...], rsa_ref[...], rsb_ref[...]

    def rot(y):
        return y * rc + pltpu.roll(y, HEAD_DIM - ROT_DIM // 2, 1) * rsa + pltpu.roll(y, ROT_DIM // 2, 1) * rsb

    qna, kna, qnm = qna_ref[...], kna_ref[...], qnm_ref[...]
    for g, q_ref in enumerate((qa0_ref, qa1_ref, qa2_ref)):
        p = _dot(hb, w_ref[:, g * GROUP_W:(g + 1) * GROUP_W])
        for h in range(HEADS_PER_GROUP):
            sl = slice(h * HEAD_DIM, (h + 1) * HEAD_DIM)
            q_ref[0, :, sl] = rot(_rms(p[:, sl], qna)).astype(BF16)
    p = _dot(hb, w_ref[:, OFF_KA:OFF_KA + GROUP_W])
    for h in range(HEADS_PER_GROUP):
        sl = slice(h * HEAD_DIM, (h + 1) * HEAD_DIM)
        k = rot(_rms(p[:, sl], kna))
        kaf_ref[0, :, sl] = k
        kab_ref[0, :, sl] = k.astype(BF16)
    p = _dot(hb, w_ref[:, OFF_VA:OFF_VA + GROUP_W])
    vaf_ref[0] = p
    vab_ref[0] = p.astype(BF16)
    p = _dot(hb, w_ref[:, OFF_U:OFF_U + 2 * CONV_CH])
    glu_ref[0] = p[:, :CONV_CH] * _sigmoid(p[:, CONV_CH:])
    p = _dot(hb, w_ref[:, OFF_QM:OFF_QM + N_HEADS_M * HEAD_DIM])
    for h in range(N_HEADS_M):
        sl = slice(h * HEAD_DIM, (h + 1) * HEAD_DIM)
        qm_ref[0, :, sl] = _rms(p[:, sl], qnm).astype(BF16)


def _rope_tables(pos):
    half = ROT_DIM // 2
    inv = jnp.float32(ROPE_THETA) ** (-jnp.arange(half, dtype=F32) / half)
    ang = pos.astype(F32)[:, None] * inv[None, :]
    cos, sin = jnp.cos(ang), jnp.sin(ang)
    n = pos.shape[0]
    rest = HEAD_DIM - ROT_DIM
    rc = jnp.concatenate([cos, cos, jnp.ones((n, rest), F32)], axis=1)
    rsa = jnp.concatenate([-sin, jnp.zeros((n, half + rest), F32)], axis=1)
    rsb = jnp.concatenate([jnp.zeros((n, half), F32), sin, jnp.zeros((n, rest), F32)], axis=1)
    return rc, rsa, rsb


def _front(x, pos, g1, w_main, qna, kna, qnm, tm):
    B, S, D = x.shape
    rc, rsa, rsb = _rope_tables(pos)
    row = lambda b, i: (b, i, 0)
    const = lambda b, i: (0, 0)
    tab = lambda b, i: (i, 0)
    blk = pl.BlockSpec((1, tm, GROUP_W), row)
    sds = lambda dt: jax.ShapeDtypeStruct((B, S, GROUP_W), dt)
    return pl.pallas_call(
        _front_kernel,
        grid=(B, S // tm),
        in_specs=[pl.BlockSpec((1, tm, D), row),
                  pl.BlockSpec((1, D), const),
                  pl.BlockSpec((D, IN_MAIN_W), const),
                  pl.BlockSpec((1, HEAD_DIM), const),
                  pl.BlockSpec((1, HEAD_DIM), const),
                  pl.BlockSpec((1, HEAD_DIM), const),
                  pl.BlockSpec((tm, HEAD_DIM), tab),
                  pl.BlockSpec((tm, HEAD_DIM), tab),
                  pl.BlockSpec((tm, HEAD_DIM), tab)],
        out_specs=[blk] * 9,
        out_shape=[sds(BF16), sds(BF16), sds(BF16), sds(BF16), sds(F32), sds(BF16), sds(F32), sds(F32), sds(BF16)],
        compiler_params=_cparams(("parallel", "parallel")),
    )(x, g1, w_main, qna, kna, qnm, rc, rsa, rsb)


def _band_kernel(q_ref, kc_ref, kp_ref, vc_ref, vp_ref, o_ref, lse_ref):
    n = pl.program_id(1)
    qi = lax.broadcasted_iota(jnp.int32, (BAND_BLK, BAND_BLK), 0)
    ki = lax.broadcasted_iota(jnp.int32, (BAND_BLK, BAND_BLK), 1)
    mask_cur = ki <= qi
    mask_prev = ki >= qi + jnp.where(n > 0, 0, BAND_BLK)
    for h in range(HEADS_PER_GROUP):
        sl = slice(h * HEAD_DIM, (h + 1) * HEAD_DIM)
        q = q_ref[0, :, sl]
        sc = jnp.where(mask_cur, _dot_nt(q, kc_ref[0, :, sl]) * SCALE, NEG)
        sp = jnp.where(mask_prev, _dot_nt(q, kp_ref[0, :, sl]) * SCALE, NEG)
        m = jnp.maximum(jnp.max(sc, axis=-1, keepdims=True), jnp.max(sp, axis=-1, keepdims=True))
        pc = jnp.exp(sc - m)
        pp = jnp.exp(sp - m)
        den = jnp.sum(pc, axis=-1, keepdims=True) + jnp.sum(pp, axis=-1, keepdims=True)
        o = _dot(pc.astype(BF16), vc_ref[0, :, sl]) + _dot(pp.astype(BF16), vp_ref[0, :, sl])
        o_ref[0, :, sl] = o / den
        lse_ref[0, :, sl] = jnp.broadcast_to(m + jnp.log(den), (BAND_BLK, HEAD_DIM))


def _band_group(q, k, v, dil):
    B, S, W = q.shape
    L = S // dil
    assert L % BAND_BLK == 0
    view = lambda t: t.reshape(B, L, dil * W)
    cur = lambda b, n, r: (b, n, r)
    prev = lambda b, n, r: (b, jnp.maximum(n - 1, 0), r)
    blk = lambda im: pl.BlockSpec((1, BAND_BLK, W), im)
    o, lse = pl.pallas_call(
        _band_kernel,
        grid=(B, L // BAND_BLK, dil),
        in_specs=[blk(cur), blk(cur), blk(prev), blk(cur), blk(prev)],
        out_specs=[blk(cur), blk(cur)],
        out_shape=[jax.ShapeDtypeStruct((B, L, dil * W), F32)] * 2,
        compiler_params=_cparams(("parallel", "parallel", "parallel")),
    )(view(q), view(k), view(k), view(v), view(v))
    return o.reshape(B, S, W), lse.reshape(B, S, W)


def _mem_kv_kernel(mem_ref, g_ref, w_ref, kn_ref, mk_ref, mv_ref):
    hb = _rms(mem_ref[0], g_ref[...]).astype(BF16)
    kv = _dot(hb, w_ref[...])
    kn = kn_ref[...]
    width = N_HEADS_M * HEAD_DIM
    for h in range(N_HEADS_M):
        sl = slice(h * HEAD_DIM, (h + 1) * HEAD_DIM)
        mk_ref[0, :, sl] = _rms(kv[:, sl], kn)
    mv_ref[0] = kv[:, width:]


def _mem_kv(mem, g_mem, w_mem_kv, kn_m):
    B, M, D = mem.shape
    width = N_HEADS_M * HEAD_DIM
    const = lambda b: (0, 0)
    return pl.pallas_call(
        _mem_kv_kernel,
        grid=(B,),
        in_specs=[pl.BlockSpec((1, M, D), lambda b: (b, 0, 0)),
                  pl.BlockSpec((1, D), const),
                  pl.BlockSpec((D, 2 * width), const),
                  pl.BlockSpec((1, HEAD_DIM), const)],
        out_specs=[pl.BlockSpec((1, M, width), lambda b: (b, 0, 0))] * 2,
        out_shape=[jax.ShapeDtypeStruct((B, M, width), F32)] * 2,
        compiler_params=_cparams(("parallel",)),
    )(mem, g_mem, w_mem_kv, kn_m)


def _conv_tail(y, bdw, lng, lnb):
    yf = y + bdw
    mu = jnp.mean(yf, axis=-1, keepdims=True)
    c = yf - mu
    var = jnp.mean(c * c, axis=-1, keepdims=True)
    yn = c * lax.rsqrt(var + EPS) * lng + lnb
    return yn * _sigmoid(yn)


def _mem_attend(q, mk, mv):
    s = _dot_nt(q, mk) * SCALE
    p = jnp.exp(s - jnp.max(s, axis=-1, keepdims=True))
    return _dot(p.astype(BF16), mv) / jnp.sum(p, axis=-1, keepdims=True)


def _prompt_mix_kernel(o0_ref, o1_ref, o2_ref, l0_ref, l1_ref, l2_ref, glu_ref, halo_ref, qm_ref,
                       mk_ref, mv_ref, wdw_ref, bdw_ref, lng_ref, lnb_ref,
                       a_ref, b_ref, m_ref, ext_ref):
    tm = glu_ref.shape[1]
    l0, l1, l2 = l0_ref[0], l1_ref[0], l2_ref[0]
    lm = jnp.maximum(jnp.maximum(l0, l1), l2)
    w0, w1, w2 = jnp.exp(l0 - lm), jnp.exp(l1 - lm), jnp.exp(l2 - lm)
    a = (w0 * o0_ref[0] + w1 * o1_ref[0] + w2 * o2_ref[0]) / (w0 + w1 + w2)
    a_ref[0] = a.astype(BF16)
    keep = jnp.where(pl.program_id(1) == 0, 0.0, 1.0)
    ext_ref[0:CONV_HALO, :] = halo_ref[0] * keep
    ext_ref[CONV_HALO:, :] = glu_ref[0]
    base = CONV_HALO - (CONV_W - 1)
    y = ext_ref[pl.ds(base, tm), :] * wdw_ref[0:1, :]
    for k in range(1, CONV_W):
        y = y + ext_ref[pl.ds(base + k, tm), :] * wdw_ref[k:k + 1, :]
    b_ref[0] = _conv_tail(y, bdw_ref[...], lng_ref[...], lnb_ref[...]).astype(BF16)
    for h in range(N_HEADS_M):
        sl = slice(h * HEAD_DIM, (h + 1) * HEAD_DIM)
        m_ref[0, :, sl] = _mem_attend(qm_ref[0, :, sl], mk_ref[0, :, sl].astype(BF16),
                                      mv_ref[0, :, sl].astype(BF16)).astype(BF16)


def _prompt_mix(os_, ls_, glu, qm, mk, mv, wdw, bdw, lng, lnb, tm):
    B, S, W = glu.shape
    M = mk.shape[1]
    row = lambda b, i: (b, i, 0)
    halo = lambda b, i: (b, jnp.maximum(i * (tm // CONV_HALO) - 1, 0), 0)
    per_b = lambda b, i: (b, 0, 0)
    const = lambda b, i: (0, 0)
    blk = pl.BlockSpec((1, tm, W), row)
    return pl.pallas_call(
        _prompt_mix_kernel,
        grid=(B, S // tm),
        in_specs=[blk] * 7 + [pl.BlockSpec((1, CONV_HALO, W), halo), blk,
                              pl.BlockSpec((1, M, W), per_b), pl.BlockSpec((1, M, W), per_b),
                              pl.BlockSpec((CONV_W, W), const), pl.BlockSpec((1, W), const),
                              pl.BlockSpec((1, W), const), pl.BlockSpec((1, W), const)],
        out_specs=[blk] * 3,
        out_shape=[jax.ShapeDtypeStruct((B, S, W), BF16)] * 3,
        scratch_shapes=[pltpu.VMEM((tm + CONV_HALO, W), F32)],
        compiler_params=_cparams(("parallel", "parallel")),
    )(*os_, *ls_, glu, glu, qm, mk, mv, wdw, bdw, lng, lnb)


SROWS = 8


def _sample_mix_kernel(q0_ref, q1_ref, q2_ref, kn_ref, vn_ref, glu_ref, qm_ref, wk_ref, wv_ref,
                       mk_ref, mv_ref, st_ref, wdw_ref, bdw_ref, lng_ref, lnb_ref,
                       a_ref, b_ref, m_ref, nst_ref):
    n_buf = wk_ref.shape[1]
    row = lax.broadcasted_iota(jnp.int32, (SROWS, n_buf), 0)
    dist = n_buf - lax.broadcasted_iota(jnp.int32, (SROWS, n_buf), 1)
    (w0, d0), (w1, d1), (w2, d2) = DIL_GROUPS
    dmask = jnp.where(row == 0, d0 - 1, jnp.where(row == 1, d1 - 1, d2 - 1))
    win = jnp.where(row == 0, w0, jnp.where(row == 1, w1, w2))
    valid = jnp.logical_and((dist & dmask) == 0, dist <= win)
    for j in range(HEADS_PER_GROUP):
        sl = slice(j * HEAD_DIM, (j + 1) * HEAD_DIM)
        q8 = jnp.where(row[:, :HEAD_DIM] == 0, q0_ref[0, :, sl],
                       jnp.where(row[:, :HEAD_DIM] == 1, q1_ref[0, :, sl], q2_ref[0, :, sl]))
        s = jnp.where(valid, _dot_nt(q8.astype(BF16), wk_ref[0, :, sl].astype(BF16)) * SCALE, NEG)
        s_new = jnp.sum(q8 * kn_ref[0, :, sl], axis=-1, keepdims=True) * SCALE
        mx = jnp.maximum(jnp.max(s, axis=-1, keepdims=True), s_new)
        p = jnp.exp(s - mx)
        p_new = jnp.exp(s_new - mx)
        den = jnp.sum(p, axis=-1, keepdims=True) + p_new
        o = (_dot(p.astype(BF16), wv_ref[0, :, sl].astype(BF16)) + p_new * vn_ref[0, :, sl]) / den
        lse = mx + jnp.log(den)
        lm = jnp.maximum(jnp.maximum(lse[0:1], lse[1:2]), lse[2:3])
        g0, g1, g2 = jnp.exp(lse[0:1] - lm), jnp.exp(lse[1:2] - lm), jnp.exp(lse[2:3] - lm)
        a_ref[0, :, sl] = (g0 * o[0:1] + g1 * o[1:2] + g2 * o[2:3]) / (g0 + g1 + g2)
        qm8 = jnp.broadcast_to(qm_ref[0, :, sl], (SROWS, HEAD_DIM)).astype(BF16)
        mo = _mem_attend(qm8, mk_ref[0, :, sl].astype(BF16), mv_ref[0, :, sl].astype(BF16))
        m_ref[0, :, sl] = mo[0:1]
    glu = glu_ref[0]
    y = jnp.sum(st_ref[0] * wdw_ref[0:CONV_W - 1, :], axis=0, keepdims=True) + glu * wdw_ref[CONV_W - 1:CONV_W, :]
    b_ref[0] = _conv_tail(y, bdw_ref[...], lng_ref[...], lnb_ref[...])
    nst_ref[0, 0:CONV_W - 2, :] = st_ref[0, 1:CONV_W - 1, :]
    nst_ref[0, CONV_W - 2:CONV_W - 1, :] = glu


def _sample_mix(qs, kn, vn, glu, qm, wk, wv, mk, mv, st, wdw, bdw, lng, lnb):
    N, n_buf, W = wk.shape
    M = mk.shape[1]
    one = pl.BlockSpec((1, 1, W), lambda b: (b, 0, 0))
    per = lambda r: pl.BlockSpec((1, r, W), lambda b: (b, 0, 0))
    const = lambda b: (0, 0)
    return pl.pallas_call(
        _sample_mix_kernel,
        grid=(N,),
        in_specs=[one] * 7 + [per(n_buf), per(n_buf), per(M), per(M), per(CONV_W - 1),
                              pl.BlockSpec((CONV_W, W), const), pl.BlockSpec((1, W), const),
                              pl.BlockSpec((1, W), const), pl.BlockSpec((1, W), const)],
        out_specs=[one, one, one, per(CONV_W - 1)],
        out_shape=[jax.ShapeDtypeStruct((N, 1, W), F32)] * 3 + [jax.ShapeDtypeStruct((N, CONV_W - 1, W), F32)],
        compiler_params=_cparams(("parallel",)),
    )(*qs, kn, vn, glu, qm, wk, wv, mk, mv, st, wdw, bdw, lng, lnb)


def _back_kernel(x_ref, a_ref, b_ref, m_ref, g1_ref, wg_ref, bg_ref, wa_ref, wb_ref, wm_ref, wo_ref, g2_ref,
                 x1_ref, h2_ref):
    x = x_ref[...]
    hb = _rms(x, g1_ref[...]).astype(BF16)
    merged = None
    for i, (t_ref, w_ref) in enumerate(((a_ref, wa_ref), (b_ref, wb_ref), (m_ref, wm_ref))):
        sl = slice(i * D_MODEL, (i + 1) * D_MODEL)
        gate = _sigmoid(_dot(hb, wg_ref[:, sl]) + bg_ref[:, sl])
        term = gate * _dot(t_ref[...], w_ref[...])
        merged = term if merged is None else merged + term
    x1 = x + _dot(merged.astype(BF16), wo_ref[...])
    x1_ref[...] = x1
    h2_ref[...] = _rms(x1, g2_ref[...]).astype(BF16)


def _back(x, a, b, m, g1, wg, bg, wa, wb, wm, wo, g2, tm):
    N, D = x.shape
    W = a.shape[1]
    row = lambda i: (i, 0)
    const = lambda i: (0, 0)
    return pl.pallas_call(
        _back_kernel,
        grid=(N // tm,),
        in_specs=[pl.BlockSpec((tm, D), row)] + [pl.BlockSpec((tm, W), row)] * 3 +
                 [pl.BlockSpec((1, D), const), pl.BlockSpec((D, 3 * D), const), pl.BlockSpec((1, 3 * D), const),
                  pl.BlockSpec((W, D), const), pl.BlockSpec((W, D), const), pl.BlockSpec((W, D), const),
                  pl.BlockSpec((D, D), const), pl.BlockSpec((1, D), const)],
        out_specs=[pl.BlockSpec((tm, D), row)] * 2,
        out_shape=[jax.ShapeDtypeStruct((N, D), F32), jax.ShapeDtypeStruct((N, D), BF16)],
        compiler_params=_cparams(("parallel",)),
    )(x, a, b, m, g1, wg, bg, wa, wb, wm, wo, g2)


PEER_CAND_ROWS = 56


def _top_sorted(s, k):
    out = []
    for _ in range(k):
        m = jnp.max(s, axis=0, keepdims=True)
        out.append(m)
        s = jnp.where(s == m, NEG, s)
    return out


def _peer_route_kernel(h_ref, wq_ref, sk_ref, s2_ref, th_ref, e1_ref, e2_ref, cand_ref):
    hb = h_ref[...]
    pairs = [(p, q) for p in range(PEER_TOPK) for q in range(PEER_TOPK) if (p + 1) * (q + 1) <= PEER_TOPK]
    cand_ref[...] = jnp.full(cand_ref.shape, NEG, F32)
    for h in range(PEER_HEADS):
        r0 = h * 2 * N_KEYS
        q1 = _dot_nt(wq_ref[r0:r0 + N_KEYS, :], hb).astype(BF16)
        q2 = _dot_nt(wq_ref[r0 + N_KEYS:r0 + 2 * N_KEYS, :], hb).astype(BF16)
        s1 = _dot(sk_ref[0], q1)
        s2 = _dot(sk_ref[1], q2)
        t1 = _top_sorted(s1, PEER_TOPK)
        t2 = _top_sorted(s2, PEER_TOPK)
        for r, (p, q) in enumerate(pairs):
            cand_ref[r:r + 1, :] = t1[p] + t2[q]
        top = _top_sorted(cand_ref[...], PEER_TOPK)
        z = jnp.exp(top[0] - top[0])
        for k in range(1, PEER_TOPK):
            z = z + jnp.exp(top[k] - top[0])
        s2_ref[h] = s2
        th = jnp.full(s1.shape, -NEG, F32)
        for q in range(PEER_TOPK):
            th = jnp.where(s1 + t2[q] >= top[PEER_TOPK - 1], t2[q], th)
        th_ref[h] = th
        e1_ref[h] = jnp.exp(s1 - t1[0])
        e2_ref[h] = jnp.exp(s2 - t2[0]) / z


def _peer_route(h2, wq_t, sk, t):
    N, D = h2.shape
    blk = pl.BlockSpec((PEER_HEADS, N_KEYS, t), lambda i: (0, 0, i))
    return pl.pallas_call(
        _peer_route_kernel,
        grid=(N // t,),
        in_specs=[pl.BlockSpec((t, D), lambda i: (i, 0)),
                  pl.BlockSpec(wq_t.shape, lambda i: (0, 0)),
                  pl.BlockSpec(sk.shape, lambda i: (0, 0, 0))],
        out_specs=[blk] * 4,
        out_shape=[jax.ShapeDtypeStruct((PEER_HEADS, N_KEYS, N), F32)] * 4,
        scratch_shapes=[pltpu.VMEM((PEER_CAND_ROWS, t), F32)],
        compiler_params=_cparams(("parallel",)),
    )(h2, wq_t, sk)


PEER_CI = 4
GELU_C = 0.7978845608028654


def _gelu_tanh(a):
    return a * (0.5 * (1.0 + jnp.tanh(GELU_C * (a + 0.044715 * (a * a * a)))))


def _peer_dense_kernel(h_ref, x1_ref, u_ref, vt_ref, s2_ref, th_ref, e1_ref, e2_ref, y_ref, acc_ref):
    c = pl.program_id(1)

    @pl.when(c == 0)
    def _():
        acc_ref[...] = jnp.zeros_like(acc_ref)

    at = _dot_nt(u_ref[...], h_ref[...])
    hs = []
    for ii in range(PEER_CI):
        i = c * PEER_CI + ii
        w = None
        for h in range(PEER_HEADS):
            sel = s2_ref[h] >= th_ref[h, pl.ds(i, 1), :]
            term = jnp.where(sel, e2_ref[h], 0.0) * e1_ref[h, pl.ds(i, 1), :]
            w = term if w is None else w + term
        a = at[ii * N_KEYS:(ii + 1) * N_KEYS, :]
        hs.append((w * _gelu_tanh(a)).astype(BF16))
    acc_ref[...] += _dot(vt_ref[...], jnp.concatenate(hs, axis=0))

    @pl.when(c == pl.num_programs(1) - 1)
    def _():
        y_ref[...] = x1_ref[...] + acc_ref[...].T


def _peer_dense(h2, x1, u_b, vt_b, s2, th, e1, e2, t):
    N, D = h2.shape
    E = u_b.shape[0]
    ce = PEER_CI * N_KEYS
    tok = lambda i, c: (i, 0)
    rt = pl.BlockSpec((PEER_HEADS, N_KEYS, t), lambda i, c: (0, 0, i))
    return pl.pallas_call(
        _peer_dense_kernel,
        grid=(N // t, E // ce),
        in_specs=[pl.BlockSpec((t, D), tok), pl.BlockSpec((t, D), tok),
                  pl.BlockSpec((ce, D), lambda i, c: (c, 0)),
                  pl.BlockSpec((D, ce), lambda i, c: (0, c)),
                  rt, rt, rt, rt],
        out_specs=pl.BlockSpec((t, D), tok),
        out_shape=jax.ShapeDtypeStruct((N, D), F32),
        scratch_shapes=[pltpu.VMEM((D, t), F32)],
        compiler_params=_cparams(("parallel", "arbitrary")),
    )(h2, x1, u_b, vt_b, s2, th, e1, e2)


def _peer(h2, x1, wq_t, sk, u_b, vt_b, t):
    N = h2.shape[0]
    pad = -N % t
    if pad:
        h2 = jnp.pad(h2, ((0, pad), (0, 0)))
        x1 = jnp.pad(x1, ((0, pad), (0, 0)))
    routed = _peer_route(h2, wq_t, sk, t)
    return _peer_dense(h2, x1, u_b, vt_b, *routed, t)[:N]


def kernel(x_prompt, x_sample, mem_prompt, cache_win_k, cache_win_v, cache_mem_k, cache_mem_v, state_conv, g_norm1, w_in, b_gate, qn_a, kn_a, qn_m, kn_m, g_mem, w_mem_kv, w_dw, b_dw, ln_g, ln_b, w_a_proj, w_b_proj, w_m_proj, w_o, g_norm2, w_pq, sub_keys, u_tab, v_tab):
    B, S, D = x_prompt.shape
    NS = x_sample.shape[0]
    assert cache_win_k.shape[0] == 1 and x_sample.shape[1] == 1
    n_win = min(DIL_GROUPS[-1][0], S)
    tm = min(512, S)
    tp = min(256, B * S)

    w_main = w_in[0, :, :IN_MAIN_W].astype(BF16)
    w_gate = w_in[0, :, IN_MAIN_W:].astype(BF16)
    row = lambda v: v.reshape(1, -1)
    g1, g2, bg = row(g_norm1[0]), row(g_norm2[0]), row(b_gate[0])
    qna, kna, qnm, knm = row(qn_a[0]), row(kn_a[0]), row(qn_m[0]), row(kn_m[0])
    bdw, lng, lnb = row(b_dw[0]), row(ln_g[0]), row(ln_b[0])
    wdw = w_dw[0]
    wa, wb, wm, wo = (w[0].astype(BF16) for w in (w_a_proj, w_b_proj, w_m_proj, w_o))
    wq_t = w_pq[0].T.astype(BF16)
    sk = sub_keys[0].astype(BF16)
    u_b = u_tab[0].astype(BF16)
    vt_b = v_tab[0].T.astype(BF16)

    q0, q1, q2, kab, kaf, vab, vaf, glu, qm = _front(x_prompt, jnp.arange(S), g1, w_main, qna, kna, qnm, tm)
    outs = [_band_group(q, kab, vab, dil) for q, (_, dil) in zip((q0, q1, q2), DIL_GROUPS)]
    mk, mv = _mem_kv(mem_prompt, row(g_mem[0]), w_mem_kv[0].astype(BF16), knm)
    a, b, m = _prompt_mix([o for o, _ in outs], [l for _, l in outs], glu, qm, mk, mv, wdw, bdw, lng, lnb, tm)
    flat = lambda t: t.reshape(B * S, -1)
    x1, h2 = _back(flat(x_prompt), flat(a), flat(b), flat(m), g1, w_gate, bg, wa, wb, wm, wo, g2, tm)
    y_prompt = _peer(h2, x1, wq_t, sk, u_b, vt_b, tp).reshape(B, S, D)

    xs = x_sample.reshape(1, NS, D)
    pos_s = jnp.full((NS,), PAST_LEN, jnp.int32)
    sq0, sq1, sq2, _, skaf, _, svaf, sglu, sqm = _front(xs, pos_s, g1, w_main, qna, kna, qnm, NS)
    tok = lambda t: t.reshape(NS, 1, -1).astype(F32)
    cw = lambda t: t[0].reshape(NS, t.shape[2], -1)
    sa, sb, sm, new_conv_s = _sample_mix(
        (tok(sq0), tok(sq1), tok(sq2)), tok(skaf), tok(svaf), tok(sglu), tok(sqm),
        cw(cache_win_k), cw(cache_win_v), cw(cache_mem_k), cw(cache_mem_v), state_conv[0],
        wdw, bdw, lng, lnb)
    mixed = lambda t: t.reshape(NS, -1).astype(BF16)
    sx1, sh2 = _back(x_sample.reshape(NS, D), mixed(sa), mixed(sb), mixed(sm),
                     g1, w_gate, bg, wa, wb, wm, wo, g2, NS)
    y_sample = _peer(sh2, sx1, wq_t, sk, u_b, vt_b, 128).reshape(NS, 1, D)

    heads = lambda t, n: t.reshape(1, t.shape[0], n, HEADS_PER_GROUP, HEAD_DIM)
    return (y_prompt, y_sample,
            heads(kaf[:, S - n_win:], n_win), heads(vaf[:, S - n_win:], n_win),
            heads(mk, N_MEM), heads(mv, N_MEM),
            glu[None, :, S - (CONV_W - 1):],
            heads(skaf.reshape(NS, 1, -1), 1), heads(svaf.reshape(NS, 1, -1), 1),
            new_conv_s[None])
```

```python
import functools

import jax
import jax.numpy as jnp
import numpy as np
from jax import lax
from jax.experimental import pallas as pl
from jax.experimental.pallas import tpu as pltpu

D_MODEL = 1024
HEAD_DIM = 128
ROT_DIM = HEAD_DIM // 4
ROPE_THETA = 500000.0
DIL_GROUPS = ((128, 1), (512, 4), (2048, 16))
HEADS_PER_GROUP = 4
N_GROUPS = len(DIL_GROUPS)
BAND_BLK = 128
CONV_CH = 512
CONV_W = 31
N_HEADS_M = 4
N_MEM = 256
PEER_HEADS = 8
N_KEYS = 128
PEER_TOPK = 16
PAST_LEN = 16384
EPS = 1e-6

GROUP_W = HEADS_PER_GROUP * HEAD_DIM
QA_W = N_GROUPS * GROUP_W
IN_MAIN_W = QA_W + 2 * GROUP_W + 2 * CONV_CH + N_HEADS_M * HEAD_DIM
OFF_KA = QA_W
OFF_VA = OFF_KA + GROUP_W
OFF_U = OFF_VA + GROUP_W
OFF_QM = OFF_U + 2 * CONV_CH
SCALE = HEAD_DIM ** -0.5
NEG = -1e30
CONV_HALO = 32

VMEM_LIMIT = 56 * 1024 * 1024
BF16 = jnp.bfloat16
F32 = jnp.float32

NT_DIMS = (((1,), (1,)), ((), ()))


def _cparams(sem):
    return pltpu.CompilerParams(dimension_semantics=sem, vmem_limit_bytes=VMEM_LIMIT)


def _rms(x, g):
    return x * lax.rsqrt(jnp.mean(x * x, axis=-1, keepdims=True) + EPS) * g


def _sigmoid(x):
    return 1.0 / (1.0 + jnp.exp(-x))


def _dot(a, b):
    return jnp.dot(a, b, preferred_element_type=F32)


def _dot_nt(a, b):
    return lax.dot_general(a, b, NT_DIMS, preferred_element_type=F32)


def _pack_rows(x):
    return pltpu.bitcast(x, jnp.int32)


def _unpack_rows(x):
    return pltpu.bitcast(x, BF16)


def _pack_rows_host(x):
    r, c = x.shape
    pairs = jnp.swapaxes(x.astype(BF16).reshape(r // 2, 2, c), 1, 2)
    return lax.bitcast_convert_type(pairs, jnp.int32)


def _front_kernel(x_ref, g1_ref, w_ref, qna_ref, kna_ref, qnm_ref, rc_ref, rsa_ref, rsb_ref,
                  qa0_ref, qa1_ref, qa2_ref, kab_ref, kaf_ref, vab_ref, vaf_ref, glu_ref, qm_ref):
    x = x_ref[0]
    hb = _rms(x, g1_ref[...]).astype(BF16)
    rc, rsa, rsb = rc_ref[...], rsa_ref[...], rsb_ref[...]

    def rot(y):
        return y * rc + pltpu.roll(y, HEAD_DIM - ROT_DIM // 2, 1) * rsa + pltpu.roll(y, ROT_DIM // 2, 1) * rsb

    qna, kna, qnm = qna_ref[...], kna_ref[...], qnm_ref[...]
    for g, q_ref in enumerate((qa0_ref, qa1_ref, qa2_ref)):
        p = _dot(hb, w_ref[:, g * GROUP_W:(g + 1) * GROUP_W])
        for h in range(HEADS_PER_GROUP):
            sl = slice(h * HEAD_DIM, (h + 1) * HEAD_DIM)
            q_ref[0, :, sl] = rot(_rms(p[:, sl], qna)).astype(BF16)
    p = _dot(hb, w_ref[:, OFF_KA:OFF_KA + GROUP_W])
    for h in range(HEADS_PER_GROUP):
        sl = slice(h * HEAD_DIM, (h + 1) * HEAD_DIM)
        k = rot(_rms(p[:, sl], kna))
        kaf_ref[0, :, sl] = k
        kab_ref[0, :, sl] = k.astype(BF16)
    p = _dot(hb, w_ref[:, OFF_VA:OFF_VA + GROUP_W])
    vaf_ref[0] = p
    vab_ref[0] = p.astype(BF16)
    p = _dot(hb, w_ref[:, OFF_U:OFF_U + 2 * CONV_CH])
    glu_ref[0] = p[:, :CONV_CH] * _sigmoid(p[:, CONV_CH:])
    p = _dot(hb, w_ref[:, OFF_QM:OFF_QM + N_HEADS_M * HEAD_DIM])
    for h in range(N_HEADS_M):
        sl = slice(h * HEAD_DIM, (h + 1) * HEAD_DIM)
        qm_ref[0, :, sl] = _rms(p[:, sl], qnm).astype(BF16)


def _rope_tables(pos):
    half = ROT_DIM // 2
    inv = jnp.float32(ROPE_THETA) ** (-jnp.arange(half, dtype=F32) / half)
    ang = pos.astype(F32)[:, None] * inv[None, :]
    cos, sin = jnp.cos(ang), jnp.sin(ang)
    n = pos.shape[0]
    rest = HEAD_DIM - ROT_DIM
    rc = jnp.concatenate([cos, cos, jnp.ones((n, rest), F32)], axis=1)
    rsa = jnp.concatenate([-sin, jnp.zeros((n, half + rest), F32)], axis=1)
    rsb = jnp.concatenate([jnp.zeros((n, half), F32), sin, jnp.zeros((n, rest), F32)], axis=1)
    return rc, rsa, rsb


def _front(x, pos, g1, w_main, qna, kna, qnm, tm):
    B, S, D = x.shape
    rc, rsa, rsb = _rope_tables(pos)
    row = lambda b, i: (b, i, 0)
    const = lambda b, i: (0, 0)
    tab = lambda b, i: (i, 0)
    blk = pl.BlockSpec((1, tm, GROUP_W), row)
    sds = lambda dt: jax.ShapeDtypeStruct((B, S, GROUP_W), dt)
    return pl.pallas_call(
        _front_kernel,
        grid=(B, S // tm),
        in_specs=[pl.BlockSpec((1, tm, D), row),
                  pl.BlockSpec((1, D), const),
                  pl.BlockSpec((D, IN_MAIN_W), const),
                  pl.BlockSpec((1, HEAD_DIM), const),
                  pl.BlockSpec((1, HEAD_DIM), const),
                  pl.BlockSpec((1, HEAD_DIM), const),
                  pl.BlockSpec((tm, HEAD_DIM), tab),
                  pl.BlockSpec((tm, HEAD_DIM), tab),
                  pl.BlockSpec((tm, HEAD_DIM), tab)],
        out_specs=[blk] * 9,
        out_shape=[sds(BF16), sds(BF16), sds(BF16), sds(BF16), sds(F32), sds(BF16), sds(F32), sds(F32), sds(BF16)],
        compiler_params=_cparams(("parallel", "parallel")),
    )(x, g1, w_main, qna, kna, qnm, rc, rsa, rsb)


def _band_kernel(q_ref, kc_ref, kp_ref, vc_ref, vp_ref, o_ref, lse_ref):
    n = pl.program_id(1)
    qi = lax.broadcasted_iota(jnp.int32, (BAND_BLK, BAND_BLK), 0)
    ki = lax.broadcasted_iota(jnp.int32, (BAND_BLK, BAND_BLK), 1)
    mask_cur = ki <= qi
    mask_prev = ki >= qi + jnp.where(n > 0, 0, BAND_BLK)
    for h in range(HEADS_PER_GROUP):
        sl = slice(h * HEAD_DIM, (h + 1) * HEAD_DIM)
        q = q_ref[0, :, sl]
        sc = jnp.where(mask_cur, _dot_nt(q, kc_ref[0, :, sl]) * SCALE, NEG)
        sp = jnp.where(mask_prev, _dot_nt(q, kp_ref[0, :, sl]) * SCALE, NEG)
        m = jnp.maximum(jnp.max(sc, axis=-1, keepdims=True), jnp.max(sp, axis=-1, keepdims=True))
        pc = jnp.exp(sc - m)
        pp = jnp.exp(sp - m)
        den = jnp.sum(pc, axis=-1, keepdims=True) + jnp.sum(pp, axis=-1, keepdims=True)
        o = _dot(pc.astype(BF16), vc_ref[0, :, sl]) + _dot(pp.astype(BF16), vp_ref[0, :, sl])
        o_ref[0, :, sl] = o / den
        lse_ref[0, :, sl] = jnp.broadcast_to(m + jnp.log(den), (BAND_BLK, HEAD_DIM))


def _band_group(q, k, v, dil):
    B, S, W = q.shape
    L = S // dil
    assert L % BAND_BLK == 0
    view = lambda t: t.reshape(B, L, dil * W)
    cur = lambda b, n, r: (b, n, r)
    prev = lambda b, n, r: (b, jnp.maximum(n - 1, 0), r)
    blk = lambda im: pl.BlockSpec((1, BAND_BLK, W), im)
    o, lse = pl.pallas_call(
        _band_kernel,
        grid=(B, L // BAND_BLK, dil),
        in_specs=[blk(cur), blk(cur), blk(prev), blk(cur), blk(prev)],
        out_specs=[blk(cur), blk(cur)],
        out_shape=[jax.ShapeDtypeStruct((B, L, dil * W), F32)] * 2,
        compiler_params=_cparams(("parallel", "parallel", "parallel")),
    )(view(q), view(k), view(k), view(v), view(v))
    return o.reshape(B, S, W), lse.reshape(B, S, W)


def _mem_kv_kernel(mem_ref, g_ref, w_ref, kn_ref, mk_ref, mv_ref):
    hb = _rms(mem_ref[0], g_ref[...]).astype(BF16)
    kv = _dot(hb, w_ref[...])
    kn = kn_ref[...]
    width = N_HEADS_M * HEAD_DIM
    for h in range(N_HEADS_M):
        sl = slice(h * HEAD_DIM, (h + 1) * HEAD_DIM)
        mk_ref[0, :, sl] = _rms(kv[:, sl], kn)
    mv_ref[0] = kv[:, width:]


def _mem_kv(mem, g_mem, w_mem_kv, kn_m):
    B, M, D = mem.shape
    width = N_HEADS_M * HEAD_DIM
    const = lambda b: (0, 0)
    return pl.pallas_call(
        _mem_kv_kernel,
        grid=(B,),
        in_specs=[pl.BlockSpec((1, M, D), lambda b: (b, 0, 0)),
                  pl.BlockSpec((1, D), const),
                  pl.BlockSpec((D, 2 * width), const),
                  pl.BlockSpec((1, HEAD_DIM), const)],
        out_specs=[pl.BlockSpec((1, M, width), lambda b: (b, 0, 0))] * 2,
        out_shape=[jax.ShapeDtypeStruct((B, M, width), F32)] * 2,
        compiler_params=_cparams(("parallel",)),
    )(mem, g_mem, w_mem_kv, kn_m)


def _conv_tail(y, bdw, lng, lnb):
    yf = y + bdw
    mu = jnp.mean(yf, axis=-1, keepdims=True)
    c = yf - mu
    var = jnp.mean(c * c, axis=-1, keepdims=True)
    yn = c * lax.rsqrt(var + EPS) * lng + lnb
    return yn * _sigmoid(yn)


def _mem_attend(q, mk, mv):
    s = _dot_nt(q, mk) * SCALE
    p = jnp.exp(s - jnp.max(s, axis=-1, keepdims=True))
    return _dot(p.astype(BF16), mv) / jnp.sum(p, axis=-1, keepdims=True)


def _prompt_mix_kernel(o0_ref, o1_ref, o2_ref, l0_ref, l1_ref, l2_ref, glu_ref, halo_ref, qm_ref,
                       mk_ref, mv_ref, wdw_ref, bdw_ref, lng_ref, lnb_ref,
                       a_ref, b_ref, m_ref, ext_ref):
    tm = glu_ref.shape[1]
    l0, l1, l2 = l0_ref[0], l1_ref[0], l2_ref[0]
    lm = jnp.maximum(jnp.maximum(l0, l1), l2)
    w0, w1, w2 = jnp.exp(l0 - lm), jnp.exp(l1 - lm), jnp.exp(l2 - lm)
    a = (w0 * o0_ref[0] + w1 * o1_ref[0] + w2 * o2_ref[0]) / (w0 + w1 + w2)
    a_ref[0] = a.astype(BF16)
    keep = jnp.where(pl.program_id(1) == 0, 0.0, 1.0)
    ext_ref[0:CONV_HALO, :] = halo_ref[0] * keep
    ext_ref[CONV_HALO:, :] = glu_ref[0]
    base = CONV_HALO - (CONV_W - 1)
    y = ext_ref[pl.ds(base, tm), :] * wdw_ref[0:1, :]
    for k in range(1, CONV_W):
        y = y + ext_ref[pl.ds(base + k, tm), :] * wdw_ref[k:k + 1, :]
    b_ref[0] = _conv_tail(y, bdw_ref[...], lng_ref[...], lnb_ref[...]).astype(BF16)
    for h in range(N_HEADS_M):
        sl = slice(h * HEAD_DIM, (h + 1) * HEAD_DIM)
        m_ref[0, :, sl] = _mem_attend(qm_ref[0, :, sl], mk_ref[0, :, sl].astype(BF16),
                                      mv_ref[0, :, sl].astype(BF16)).astype(BF16)


def _prompt_mix(os_, ls_, glu, qm, mk, mv, wdw, bdw, lng, lnb, tm):
    B, S, W = glu.shape
    M = mk.shape[1]
    row = lambda b, i: (b, i, 0)
    halo = lambda b, i: (b, jnp.maximum(i * (tm // CONV_HALO) - 1, 0), 0)
    per_b = lambda b, i: (b, 0, 0)
    const = lambda b, i: (0, 0)
    blk = pl.BlockSpec((1, tm, W), row)
    return pl.pallas_call(
        _prompt_mix_kernel,
        grid=(B, S // tm),
        in_specs=[blk] * 7 + [pl.BlockSpec((1, CONV_HALO, W), halo), blk,
                              pl.BlockSpec((1, M, W), per_b), pl.BlockSpec((1, M, W), per_b),
                              pl.BlockSpec((CONV_W, W), const), pl.BlockSpec((1, W), const),
                              pl.BlockSpec((1, W), const), pl.BlockSpec((1, W), const)],
        out_specs=[blk] * 3,
        out_shape=[jax.ShapeDtypeStruct((B, S, W), BF16)] * 3,
        scratch_shapes=[pltpu.VMEM((tm + CONV_HALO, W), F32)],
        compiler_params=_cparams(("parallel", "parallel")),
    )(*os_, *ls_, glu, glu, qm, mk, mv, wdw, bdw, lng, lnb)


SROWS = 8


def _sample_mix_kernel(q0_ref, q1_ref, q2_ref, kn_ref, vn_ref, glu_ref, qm_ref, wk_ref, wv_ref,
                       mk_ref, mv_ref, st_ref, wdw_ref, bdw_ref, lng_ref, lnb_ref,
                       a_ref, b_ref, m_ref, nst_ref):
    n_buf = wk_ref.shape[1]
    row = lax.broadcasted_iota(jnp.int32, (SROWS, n_buf), 0)
    dist = n_buf - lax.broadcasted_iota(jnp.int32, (SROWS, n_buf), 1)
    (w0, d0), (w1, d1), (w2, d2) = DIL_GROUPS
    dmask = jnp.where(row == 0, d0 - 1, jnp.where(row == 1, d1 - 1, d2 - 1))
    win = jnp.where(row == 0, w0, jnp.where(row == 1, w1, w2))
    valid = jnp.logical_and((dist & dmask) == 0, dist <= win)
    for j in range(HEADS_PER_GROUP):
        sl = slice(j * HEAD_DIM, (j + 1) * HEAD_DIM)
        q8 = jnp.where(row[:, :HEAD_DIM] == 0, q0_ref[0, :, sl],
                       jnp.where(row[:, :HEAD_DIM] == 1, q1_ref[0, :, sl], q2_ref[0, :, sl]))
        s = jnp.where(valid, _dot_nt(q8.astype(BF16), wk_ref[0, :, sl].astype(BF16)) * SCALE, NEG)
        s_new = jnp.sum(q8 * kn_ref[0, :, sl], axis=-1, keepdims=True) * SCALE
        mx = jnp.maximum(jnp.max(s, axis=-1, keepdims=True), s_new)
        p = jnp.exp(s - mx)
        p_new = jnp.exp(s_new - mx)
        den = jnp.sum(p, axis=-1, keepdims=True) + p_new
        o = (_dot(p.astype(BF16), wv_ref[0, :, sl].astype(BF16)) + p_new * vn_ref[0, :, sl]) / den
        lse = mx + jnp.log(den)
        lm = jnp.maximum(jnp.maximum(lse[0:1], lse[1:2]), lse[2:3])
        g0, g1, g2 = jnp.exp(lse[0:1] - lm), jnp.exp(lse[1:2] - lm), jnp.exp(lse[2:3] - lm)
        a_ref[0, :, sl] = (g0 * o[0:1] + g1 * o[1:2] + g2 * o[2:3]) / (g0 + g1 + g2)
        qm8 = jnp.broadcast_to(qm_ref[0, :, sl], (SROWS, HEAD_DIM)).astype(BF16)
        mo = _mem_attend(qm8, mk_ref[0, :, sl].astype(BF16), mv_ref[0, :, sl].astype(BF16))
        m_ref[0, :, sl] = mo[0:1]
    glu = glu_ref[0]
    y = jnp.sum(st_ref[0] * wdw_ref[0:CONV_W - 1, :], axis=0, keepdims=True) + glu * wdw_ref[CONV_W - 1:CONV_W, :]
    b_ref[0] = _conv_tail(y, bdw_ref[...], lng_ref[...], lnb_ref[...])
    nst_ref[0, 0:CONV_W - 2, :] = st_ref[0, 1:CONV_W - 1, :]
    nst_ref[0, CONV_W - 2:CONV_W - 1, :] = glu


def _sample_mix(qs, kn, vn, glu, qm, wk, wv, mk, mv, st, wdw, bdw, lng, lnb):
    N, n_buf, W = wk.shape
    M = mk.shape[1]
    one = pl.BlockSpec((1, 1, W), lambda b: (b, 0, 0))
    per = lambda r: pl.BlockSpec((1, r, W), lambda b: (b, 0, 0))
    const = lambda b: (0, 0)
    return pl.pallas_call(
        _sample_mix_kernel,
        grid=(N,),
        in_specs=[one] * 7 + [per(n_buf), per(n_buf), per(M), per(M), per(CONV_W - 1),
                              pl.BlockSpec((CONV_W, W), const), pl.BlockSpec((1, W), const),
                              pl.BlockSpec((1, W), const), pl.BlockSpec((1, W), const)],
        out_specs=[one, one, one, per(CONV_W - 1)],
        out_shape=[jax.ShapeDtypeStruct((N, 1, W), F32)] * 3 + [jax.ShapeDtypeStruct((N, CONV_W - 1, W), F32)],
        compiler_params=_cparams(("parallel",)),
    )(*qs, kn, vn, glu, qm, wk, wv, mk, mv, st, wdw, bdw, lng, lnb)


def _back_kernel(x_ref, a_ref, b_ref, m_ref, g1_ref, wg_ref, bg_ref, wa_ref, wb_ref, wm_ref, wo_ref, g2_ref,
                 x1_ref, h2_ref):
    x = x_ref[...]
    hb = _rms(x, g1_ref[...]).astype(BF16)
    merged = None
    for i, (t_ref, w_ref) in enumerate(((a_ref, wa_ref), (b_ref, wb_ref), (m_ref, wm_ref))):
        sl = slice(i * D_MODEL, (i + 1) * D_MODEL)
        gate = _sigmoid(_dot(hb, wg_ref[:, sl]) + bg_ref[:, sl])
        term = gate * _dot(t_ref[...], w_ref[...])
        merged = term if merged is None else merged + term
    x1 = x + _dot(merged.astype(BF16), wo_ref[...])
    x1_ref[...] = x1
    h2_ref[...] = _pack_rows(_rms(x1, g2_ref[...]).astype(BF16))


def _back(x, a, b, m, g1, wg, bg, wa, wb, wm, wo, g2, tm):
    N, D = x.shape
    W = a.shape[1]
    row = lambda i: (i, 0)
    const = lambda i: (0, 0)
    return pl.pallas_call(
        _back_kernel,
        grid=(N // tm,),
        in_specs=[pl.BlockSpec((tm, D), row)] + [pl.BlockSpec((tm, W), row)] * 3 +
                 [pl.BlockSpec((1, D), const), pl.BlockSpec((D, 3 * D), const), pl.BlockSpec((1, 3 * D), const),
                  pl.BlockSpec((W, D), const), pl.BlockSpec((W, D), const), pl.BlockSpec((W, D), const),
                  pl.BlockSpec((D, D), const), pl.BlockSpec((1, D), const)],
        out_specs=[pl.BlockSpec((tm, D), row), pl.BlockSpec((tm // 2, D), row)],
        out_shape=[jax.ShapeDtypeStruct((N, D), F32), jax.ShapeDtypeStruct((N // 2, D), jnp.int32)],
        compiler_params=_cparams(("parallel",)),
    )(x, a, b, m, g1, wg, bg, wa, wb, wm, wo, g2)


PEER_CAND_ROWS = 56


def _top_sorted(s, k):
    out = []
    for _ in range(k):
        m = jnp.max(s, axis=0, keepdims=True)
        out.append(m)
        s = jnp.where(s == m, NEG, s)
    return out


def _prefix_count(test, t):
    c8 = test(t[7])
    c4 = test(jnp.where(c8, t[11], t[3]))
    c2 = test(jnp.where(c8, jnp.where(c4, t[13], t[9]), jnp.where(c4, t[5], t[1])))
    c1 = test(jnp.where(c8, jnp.where(c4, jnp.where(c2, t[14], t[12]), jnp.where(c2, t[10], t[8])),
                        jnp.where(c4, jnp.where(c2, t[6], t[4]), jnp.where(c2, t[2], t[0]))))
    n = (jnp.where(c8, 8.0, 0.0) + jnp.where(c4, 4.0, 0.0)) + (jnp.where(c2, 2.0, 0.0) + jnp.where(c1, 1.0, 0.0))
    return jnp.where(test(t[15]), 16.0, n)


def _peer_route_kernel(h_ref, wq_ref, sk_ref, r2_ref, e2_ref, qd_ref, e1d_ref, cand_ref):
    hb = _unpack_rows(h_ref[...])
    pairs = [(p, q) for p in range(PEER_TOPK) for q in range(PEER_TOPK) if (p + 1) * (q + 1) <= PEER_TOPK]
    cand_ref[...] = jnp.full(cand_ref.shape, NEG, F32)
    for h in range(PEER_HEADS):
        r0 = h * 2 * N_KEYS
        q1 = _dot_nt(wq_ref[r0:r0 + N_KEYS, :], hb).astype(BF16)
        q2 = _dot_nt(wq_ref[r0 + N_KEYS:r0 + 2 * N_KEYS, :], hb).astype(BF16)
        s1 = _dot(sk_ref[0], q1)
        s2 = _dot(sk_ref[1], q2)
        t1 = _top_sorted(s1, PEER_TOPK)
        t2 = _top_sorted(s2, PEER_TOPK)
        for r, (p, q) in enumerate(pairs):
            cand_ref[r:r + 1, :] = t1[p] + t2[q]
        top = _top_sorted(cand_ref[...], PEER_TOPK)
        z = jnp.exp(top[0] - top[0])
        for k in range(1, PEER_TOPK):
            z = z + jnp.exp(top[k] - top[0])
        tau = top[PEER_TOPK - 1]
        r2_ref[h] = _pack_rows(_prefix_count(lambda v: v > s2, t2).astype(BF16))
        qd_ref[h] = _prefix_count(lambda v: s1 + v >= tau, t2)
        e1d_ref[h] = jnp.exp(s1 - t1[0])
        e2_ref[h] = _pack_rows((jnp.exp(s2 - t2[0]) * (0.5 / z)).astype(BF16))


def _peer_route(h2, wq_t, sk, t):
    N, D = 2 * h2.shape[0], h2.shape[1]
    blk = lambda r: pl.BlockSpec((PEER_HEADS, r, t), lambda i: (0, 0, i))
    sds = lambda r: jax.ShapeDtypeStruct((PEER_HEADS, r, N), jnp.int32 if r < N_KEYS else F32)
    return pl.pallas_call(
        _peer_route_kernel,
        grid=(N // t,),
        in_specs=[pl.BlockSpec((t // 2, D), lambda i: (i, 0)),
                  pl.BlockSpec(wq_t.shape, lambda i: (0, 0)),
                  pl.BlockSpec(sk.shape, lambda i: (0, 0, 0))],
        out_specs=[blk(N_KEYS // 2), blk(N_KEYS // 2), blk(N_KEYS), blk(N_KEYS)],
        out_shape=[sds(N_KEYS // 2), sds(N_KEYS // 2), sds(N_KEYS), sds(N_KEYS)],
        scratch_shapes=[pltpu.VMEM((PEER_CAND_ROWS, t), F32)],
        compiler_params=_cparams(("parallel",)),
    )(h2, wq_t, sk)


PEER_CI = 16
PEER_SUB = 4
PEER_LANES = 256
GELU_C = 0.7978845608028654
GELU_A = 0.044715


def _peer_dense_kernel(h_ref, x1_ref, u_ref, vt_ref, r2_ref, e2_ref, qd_ref, e1d_ref, y_ref, acc_ref):
    c = pl.program_id(1)
    t = x1_ref.shape[0]
    lanes = min(PEER_LANES, t)
    sub = PEER_SUB * N_KEYS
    n_sub = PEER_CI // PEER_SUB

    @pl.when(c == 0)
    def _():
        acc_ref[...] = jnp.zeros_like(acc_ref)

    def row_bf16(ref, h, i, ls):
        return jnp.broadcast_to(ref[h, pl.ds(i, 1), ls], (N_KEYS, lanes)).astype(BF16)

    zero = jnp.zeros((N_KEYS, lanes), BF16)
    subs = [(l0, q) for l0 in range(0, t, lanes) for q in range(n_sub)]

    def first_matmul(k):
        l0, q = subs[k]
        u = _unpack_rows(u_ref[q * sub // 2:(q + 1) * sub // 2, :])
        return _dot_nt(u, _unpack_rows(h_ref[l0 // 2:(l0 + lanes) // 2, :]))

    def gated(k, at):
        l0, q = subs[k]
        ls = slice(l0, l0 + lanes)
        rows = []
        for ii in range(PEER_SUB):
            i = c * PEER_CI + q * PEER_SUB + ii
            w = None
            for h in range(PEER_HEADS):
                sel = _unpack_rows(r2_ref[h, :, ls]) < row_bf16(qd_ref, h, i, ls)
                term = jnp.where(sel, _unpack_rows(e2_ref[h, :, ls]), zero) * row_bf16(e1d_ref, h, i, ls)
                w = term if w is None else w + term
            a = at[ii * N_KEYS:(ii + 1) * N_KEYS, :]
            inner = a * ((a * a) * (GELU_C * GELU_A) + GELU_C)
            rows.append(w * (a * jnp.tanh(inner) + a).astype(BF16))
        return jnp.concatenate(rows, axis=0)

    parts = {}

    def second_matmul(k, hk):
        l0, q = subs[k]
        d = _dot(_unpack_rows(vt_ref[:, q * sub:(q + 1) * sub]), hk)
        parts[l0] = d if q == 0 else parts[l0] + d
        if q == n_sub - 1:
            acc_ref[:, l0:l0 + lanes] += parts[l0]

    at_next = first_matmul(0)
    h_prev = None
    for k in range(len(subs)):
        at = at_next
        if k + 1 < len(subs):
            at_next = first_matmul(k + 1)
        if k > 0:
            second_matmul(k - 1, h_prev)
        h_prev = gated(k, at)
    second_matmul(len(subs) - 1, h_prev)

    @pl.when(c == pl.num_programs(1) - 1)
    def _():
        y_ref[...] = x1_ref[...] + acc_ref[...].T


def _peer_dense(h2, x1, u_p, vt_p, r2, e2, qd, e1d, t):
    N, D = x1.shape
    E = 2 * u_p.shape[0]
    ce = PEER_CI * N_KEYS
    tok = lambda i, c: (i, 0)
    rt = lambda r: pl.BlockSpec((PEER_HEADS, r, t), lambda i, c: (0, 0, i))
    return pl.pallas_call(
        _peer_dense_kernel,
        grid=(N // t, E // ce),
        in_specs=[pl.BlockSpec((t // 2, D), tok), pl.BlockSpec((t, D), tok),
                  pl.BlockSpec((ce // 2, D), lambda i, c: (c, 0)),
                  pl.BlockSpec((D // 2, ce), lambda i, c: (0, c)),
                  rt(N_KEYS // 2), rt(N_KEYS // 2), rt(N_KEYS), rt(N_KEYS)],
        out_specs=pl.BlockSpec((t, D), tok),
        out_shape=jax.ShapeDtypeStruct((N, D), F32),
        scratch_shapes=[pltpu.VMEM((D, t), F32)],
        compiler_params=_cparams(("parallel", "arbitrary")),
    )(h2, x1, u_p, vt_p, r2, e2, qd, e1d)


def _peer(h2, x1, wq_t, sk, u_p, vt_p, t):
    N = x1.shape[0]
    pad = -N % t
    if pad:
        h2 = jnp.pad(h2, ((0, pad // 2), (0, 0)))
        x1 = jnp.pad(x1, ((0, pad), (0, 0)))
    routed = _peer_route(h2, wq_t, sk, min(t, PEER_LANES))
    return _peer_dense(h2, x1, u_p, vt_p, *routed, t)[:N]


def kernel(x_prompt, x_sample, mem_prompt, cache_win_k, cache_win_v, cache_mem_k, cache_mem_v, state_conv, g_norm1, w_in, b_gate, qn_a, kn_a, qn_m, kn_m, g_mem, w_mem_kv, w_dw, b_dw, ln_g, ln_b, w_a_proj, w_b_proj, w_m_proj, w_o, g_norm2, w_pq, sub_keys, u_tab, v_tab):
    B, S, D = x_prompt.shape
    NS = x_sample.shape[0]
    assert cache_win_k.shape[0] == 1 and x_sample.shape[1] == 1
    n_win = min(DIL_GROUPS[-1][0], S)
    tm = min(512, S)
    tp = min(512, B * S)

    w_main = w_in[0, :, :IN_MAIN_W].astype(BF16)
    w_gate = w_in[0, :, IN_MAIN_W:].astype(BF16)
    row = lambda v: v.reshape(1, -1)
    g1, g2, bg = row(g_norm1[0]), row(g_norm2[0]), row(b_gate[0])
    qna, kna, qnm, knm = row(qn_a[0]), row(kn_a[0]), row(qn_m[0]), row(kn_m[0])
    bdw, lng, lnb = row(b_dw[0]), row(ln_g[0]), row(ln_b[0])
    wdw = w_dw[0]
    wa, wb, wm, wo = (w[0].astype(BF16) for w in (w_a_proj, w_b_proj, w_m_proj, w_o))
    wq_t = w_pq[0].T.astype(BF16)
    sk = sub_keys[0].astype(BF16)
    u_b = _pack_rows_host(u_tab[0])
    vt_b = _pack_rows_host(v_tab[0].T)

    q0, q1, q2, kab, kaf, vab, vaf, glu, qm = _front(x_prompt, jnp.arange(S), g1, w_main, qna, kna, qnm, tm)
    outs = [_band_group(q, kab, vab, dil) for q, (_, dil) in zip((q0, q1, q2), DIL_GROUPS)]
    mk, mv = _mem_kv(mem_prompt, row(g_mem[0]), w_mem_kv[0].astype(BF16), knm)
    a, b, m = _prompt_mix([o for o, _ in outs], [l for _, l in outs], glu, qm, mk, mv, wdw, bdw, lng, lnb, tm)
    flat = lambda t: t.reshape(B * S, -1)
    x1, h2 = _back(flat(x_prompt), flat(a), flat(b), flat(m), g1, w_gate, bg, wa, wb, wm, wo, g2, tm)
    y_prompt = _peer(h2, x1, wq_t, sk, u_b, vt_b, tp).reshape(B, S, D)

    xs = x_sample.reshape(1, NS, D)
    pos_s = jnp.full((NS,), PAST_LEN, jnp.int32)
    sq0, sq1, sq2, _, skaf, _, svaf, sglu, sqm = _front(xs, pos_s, g1, w_main, qna, kna, qnm, NS)
    tok = lambda t: t.reshape(NS, 1, -1).astype(F32)
    cw = lambda t: t[0].reshape(NS, t.shape[2], -1)
    sa, sb, sm, new_conv_s = _sample_mix(
        (tok(sq0), tok(sq1), tok(sq2)), tok(skaf), tok(svaf), tok(sglu), tok(sqm),
        cw(cache_win_k), cw(cache_win_v), cw(cache_mem_k), cw(cache_mem_v), state_conv[0],
        wdw, bdw, lng, lnb)
    mixed = lambda t: t.reshape(NS, -1).astype(BF16)
    sx1, sh2 = _back(x_sample.reshape(NS, D), mixed(sa), mixed(sb), mixed(sm),
                     g1, w_gate, bg, wa, wb, wm, wo, g2, NS)
    y_sample = _peer(sh2, sx1, wq_t, sk, u_b, vt_b, 128).reshape(NS, 1, D)

    heads = lambda t, n: t.reshape(1, t.shape[0], n, HEADS_PER_GROUP, HEAD_DIM)
    return (y_prompt, y_sample,
            heads(kaf[:, S - n_win:], n_win), heads(vaf[:, S - n_win:], n_win),
            heads(mk, N_MEM), heads(mv, N_MEM),
            glu[None, :, S - (CONV_W - 1):],
            heads(skaf.reshape(NS, 1, -1), 1), heads(svaf.reshape(NS, 1, -1), 1),
            new_conv_s[None])
```

```python
import functools

import jax
import jax.numpy as jnp
import numpy as np
from jax import lax
from jax.experimental import pallas as pl
from jax.experimental.pallas import tpu as pltpu

D_MODEL = 1024
HEAD_DIM = 128
ROT_DIM = HEAD_DIM // 4
ROPE_THETA = 500000.0
DIL_GROUPS = ((128, 1), (512, 4), (2048, 16))
HEADS_PER_GROUP = 4
N_GROUPS = len(DIL_GROUPS)
BAND_BLK = 128
CONV_CH = 512
CONV_W = 31
N_HEADS_M = 4
N_MEM = 256
PEER_HEADS = 8
N_KEYS = 128
PEER_TOPK = 16
PAST_LEN = 16384
EPS = 1e-6

GROUP_W = HEADS_PER_GROUP * HEAD_DIM
QA_W = N_GROUPS * GROUP_W
IN_MAIN_W = QA_W + 2 * GROUP_W + 2 * CONV_CH + N_HEADS_M * HEAD_DIM
OFF_KA = QA_W
OFF_VA = OFF_KA + GROUP_W
OFF_U = OFF_VA + GROUP_W
OFF_QM = OFF_U + 2 * CONV_CH
SCALE = HEAD_DIM ** -0.5
NEG = -1e30
CONV_HALO = 32
BAND_STEP_ROWS = 512

VMEM_LIMIT = 56 * 1024 * 1024
BF16 = jnp.bfloat16
F32 = jnp.float32

NT_DIMS = (((1,), (1,)), ((), ()))


def _cparams(sem):
    return pltpu.CompilerParams(dimension_semantics=sem, vmem_limit_bytes=VMEM_LIMIT)


def _rms(x, g):
    return x * lax.rsqrt(jnp.mean(x * x, axis=-1, keepdims=True) + EPS) * g


def _sigmoid(x):
    return 1.0 / (1.0 + jnp.exp(-x))


def _dot(a, b):
    return jnp.dot(a, b, preferred_element_type=F32)


def _dot_nt(a, b):
    return lax.dot_general(a, b, NT_DIMS, preferred_element_type=F32)


def _pack_rows(x):
    return pltpu.bitcast(x, jnp.int32)


def _unpack_rows(x):
    return pltpu.bitcast(x, BF16)


def _pack_table_kernel(x_ref, o_ref, *, transpose):
    x = x_ref[...]
    o_ref[...] = _pack_rows((x.T if transpose else x).astype(BF16))


def _pack_table(x, transpose, rows=512):
    R, C = x.shape
    if transpose:
        out_shape, out_spec = (C // 2, R), pl.BlockSpec((C // 2, rows), lambda i: (0, i))
    else:
        out_shape, out_spec = (R // 2, C), pl.BlockSpec((rows // 2, C), lambda i: (i, 0))
    return pl.pallas_call(
        functools.partial(_pack_table_kernel, transpose=transpose),
        grid=(R // rows,),
        in_specs=[pl.BlockSpec((rows, C), lambda i: (i, 0))],
        out_specs=out_spec,
        out_shape=jax.ShapeDtypeStruct(out_shape, jnp.int32),
        compiler_params=_cparams(("parallel",)),
    )(x)


def _front_kernel(x_ref, g1_ref, w_ref, qna_ref, kna_ref, qnm_ref, rc_ref, rsa_ref, rsb_ref,
                  qa0_ref, qa1_ref, qa2_ref, kaf_ref, vaf_ref, glu_ref, qm_ref):
    x = x_ref[0]
    hb = _rms(x, g1_ref[...]).astype(BF16)
    rc, rsa, rsb = rc_ref[...], rsa_ref[...], rsb_ref[...]

    def rot(y):
        return y * rc + pltpu.roll(y, HEAD_DIM - ROT_DIM // 2, 1) * rsa + pltpu.roll(y, ROT_DIM // 2, 1) * rsb

    qna, kna, qnm = qna_ref[...], kna_ref[...], qnm_ref[...]
    for g, q_ref in enumerate((qa0_ref, qa1_ref, qa2_ref)):
        p = _dot(hb, w_ref[:, g * GROUP_W:(g + 1) * GROUP_W])
        for h in range(HEADS_PER_GROUP):
            sl = slice(h * HEAD_DIM, (h + 1) * HEAD_DIM)
            q_ref[0, :, sl] = rot(_rms(p[:, sl], qna))
    p = _dot(hb, w_ref[:, OFF_KA:OFF_KA + GROUP_W])
    for h in range(HEADS_PER_GROUP):
        sl = slice(h * HEAD_DIM, (h + 1) * HEAD_DIM)
        kaf_ref[0, :, sl] = rot(_rms(p[:, sl], kna))
    vaf_ref[0] = _dot(hb, w_ref[:, OFF_VA:OFF_VA + GROUP_W])
    p = _dot(hb, w_ref[:, OFF_U:OFF_U + 2 * CONV_CH])
    glu_ref[0] = p[:, :CONV_CH] * _sigmoid(p[:, CONV_CH:])
    p = _dot(hb, w_ref[:, OFF_QM:OFF_QM + N_HEADS_M * HEAD_DIM])
    for h in range(N_HEADS_M):
        sl = slice(h * HEAD_DIM, (h + 1) * HEAD_DIM)
        qm_ref[0, :, sl] = _rms(p[:, sl], qnm).astype(BF16)


def _rope_tables(pos):
    half = ROT_DIM // 2
    inv = jnp.float32(ROPE_THETA) ** (-jnp.arange(half, dtype=F32) / half)
    ang = pos.astype(F32)[:, None] * inv[None, :]
    cos, sin = jnp.cos(ang), jnp.sin(ang)
    n = pos.shape[0]
    rest = HEAD_DIM - ROT_DIM
    rc = jnp.concatenate([cos, cos, jnp.ones((n, rest), F32)], axis=1)
    rsa = jnp.concatenate([-sin, jnp.zeros((n, half + rest), F32)], axis=1)
    rsb = jnp.concatenate([jnp.zeros((n, half), F32), sin, jnp.zeros((n, rest), F32)], axis=1)
    return rc, rsa, rsb


def _front(x, pos, g1, w_main, qna, kna, qnm, tm):
    B, S, D = x.shape
    rc, rsa, rsb = _rope_tables(pos)
    row = lambda b, i: (b, i, 0)
    const = lambda b, i: (0, 0)
    tab = lambda b, i: (i, 0)
    blk = pl.BlockSpec((1, tm, GROUP_W), row)
    sds = lambda dt: jax.ShapeDtypeStruct((B, S, GROUP_W), dt)
    return pl.pallas_call(
        _front_kernel,
        grid=(B, S // tm),
        in_specs=[pl.BlockSpec((1, tm, D), row),
                  pl.BlockSpec((1, D), const),
                  pl.BlockSpec((D, IN_MAIN_W), const),
                  pl.BlockSpec((1, HEAD_DIM), const),
                  pl.BlockSpec((1, HEAD_DIM), const),
                  pl.BlockSpec((1, HEAD_DIM), const),
                  pl.BlockSpec((tm, HEAD_DIM), tab),
                  pl.BlockSpec((tm, HEAD_DIM), tab),
                  pl.BlockSpec((tm, HEAD_DIM), tab)],
        out_specs=[blk] * 7,
        out_shape=[sds(F32)] * 6 + [sds(BF16)],
        compiler_params=_cparams(("parallel", "parallel")),
    )(x, g1, w_main, qna, kna, qnm, rc, rsa, rsb)


def _band_kernel(q_ref, kc_ref, kp_ref, vc_ref, vp_ref, o_ref, lse_ref, *, dil):
    span = dil * BAND_BLK
    n_units = q_ref.shape[1] // span
    qi = lax.broadcasted_iota(jnp.int32, (BAND_BLK, BAND_BLK), 0)
    ki = lax.broadcasted_iota(jnp.int32, (BAND_BLK, BAND_BLK), 1)
    mask_cur = ki <= qi
    mask_prev = ki >= qi
    mask_first = ki >= qi + jnp.where(pl.program_id(1) > 0, 0, BAND_BLK)

    def rows(ref, r, u):
        return ref[0, pl.ds(r + u * span, BAND_BLK, stride=dil), :] if dil > 1 else ref[0, pl.ds(u * span, BAND_BLK), :]

    for r in range(dil):
        for u in range(n_units):
            q = rows(q_ref, r, u).astype(BF16)
            kc, vc = rows(kc_ref, r, u).astype(BF16), rows(vc_ref, r, u).astype(BF16)
            if u > 0:
                kp, vp, mp = rows(kc_ref, r, u - 1).astype(BF16), rows(vc_ref, r, u - 1).astype(BF16), mask_prev
            else:
                kp, vp, mp = rows(kp_ref, r, 0).astype(BF16), rows(vp_ref, r, 0).astype(BF16), mask_first
            sc = jnp.where(mask_cur, _dot_nt(q, kc) * SCALE, NEG)
            sp = jnp.where(mp, _dot_nt(q, kp) * SCALE, NEG)
            m = jnp.maximum(jnp.max(sc, axis=-1, keepdims=True), jnp.max(sp, axis=-1, keepdims=True))
            pc = jnp.exp(sc - m)
            pp = jnp.exp(sp - m)
            den = jnp.sum(pc, axis=-1, keepdims=True) + jnp.sum(pp, axis=-1, keepdims=True)
            o = (_dot(pc.astype(BF16), vc) + _dot(pp.astype(BF16), vp)) / den
            lse = jnp.broadcast_to(m + jnp.log(den), (BAND_BLK, HEAD_DIM))
            if dil > 1:
                o_ref[0, pl.ds(r + u * span, BAND_BLK, stride=dil), :] = o
                lse_ref[0, pl.ds(r + u * span, BAND_BLK, stride=dil), :] = lse
            else:
                o_ref[0, pl.ds(u * span, BAND_BLK), :] = o
                lse_ref[0, pl.ds(u * span, BAND_BLK), :] = lse


def _band_group(q, k, v, dil, rows_per_step):
    B, S, W = q.shape
    span = dil * BAND_BLK
    rb = min(rows_per_step, S)
    assert rb % span == 0 and S % rb == 0
    cur = lambda b, i, h: (b, i, h)
    prev = lambda b, i, h: (b, jnp.maximum(i * (rb // span) - 1, 0), h)
    blk = pl.BlockSpec((1, rb, HEAD_DIM), cur)
    pblk = pl.BlockSpec((1, span, HEAD_DIM), prev)
    return pl.pallas_call(
        functools.partial(_band_kernel, dil=dil),
        grid=(B, S // rb, W // HEAD_DIM),
        in_specs=[blk, blk, pblk, blk, pblk],
        out_specs=[blk, blk],
        out_shape=[jax.ShapeDtypeStruct((B, S, W), F32)] * 2,
        compiler_params=_cparams(("parallel", "parallel", "parallel")),
    )(q, k, k, v, v)


def _mem_kv_kernel(mem_ref, g_ref, w_ref, kn_ref, mk_ref, mv_ref):
    hb = _rms(mem_ref[0], g_ref[...]).astype(BF16)
    kv = _dot(hb, w_ref[...])
    kn = kn_ref[...]
    width = N_HEADS_M * HEAD_DIM
    for h in range(N_HEADS_M):
        sl = slice(h * HEAD_DIM, (h + 1) * HEAD_DIM)
        mk_ref[0, :, sl] = _rms(kv[:, sl], kn)
    mv_ref[0] = kv[:, width:]


def _mem_kv(mem, g_mem, w_mem_kv, kn_m):
    B, M, D = mem.shape
    width = N_HEADS_M * HEAD_DIM
    const = lambda b: (0, 0)
    return pl.pallas_call(
        _mem_kv_kernel,
        grid=(B,),
        in_specs=[pl.BlockSpec((1, M, D), lambda b: (b, 0, 0)),
                  pl.BlockSpec((1, D), const),
                  pl.BlockSpec((D, 2 * width), const),
                  pl.BlockSpec((1, HEAD_DIM), const)],
        out_specs=[pl.BlockSpec((1, M, width), lambda b: (b, 0, 0))] * 2,
        out_shape=[jax.ShapeDtypeStruct((B, M, width), F32)] * 2,
        compiler_params=_cparams(("parallel",)),
    )(mem, g_mem, w_mem_kv, kn_m)


def _conv_tail(y, bdw, lng, lnb):
    yf = y + bdw
    mu = jnp.mean(yf, axis=-1, keepdims=True)
    c = yf - mu
    var = jnp.mean(c * c, axis=-1, keepdims=True)
    yn = c * lax.rsqrt(var + EPS) * lng + lnb
    return yn * _sigmoid(yn)


def _mem_attend(q, mk, mv):
    s = _dot_nt(q, mk) * SCALE
    p = jnp.exp(s - jnp.max(s, axis=-1, keepdims=True))
    return _dot(p.astype(BF16), mv) / jnp.sum(p, axis=-1, keepdims=True)


def _prompt_mix_kernel(o0_ref, o1_ref, o2_ref, l0_ref, l1_ref, l2_ref, glu_ref, halo_ref, qm_ref,
                       mk_ref, mv_ref, wdw_ref, bdw_ref, lng_ref, lnb_ref,
                       a_ref, b_ref, m_ref, ext_ref):
    tm = glu_ref.shape[1]
    l0, l1, l2 = l0_ref[0], l1_ref[0], l2_ref[0]
    lm = jnp.maximum(jnp.maximum(l0, l1), l2)
    w0, w1, w2 = jnp.exp(l0 - lm), jnp.exp(l1 - lm), jnp.exp(l2 - lm)
    a = (w0 * o0_ref[0] + w1 * o1_ref[0] + w2 * o2_ref[0]) / (w0 + w1 + w2)
    a_ref[0] = a.astype(BF16)
    keep = jnp.where(pl.program_id(1) == 0, 0.0, 1.0)
    ext_ref[0:CONV_HALO, :] = halo_ref[0] * keep
    ext_ref[CONV_HALO:, :] = glu_ref[0]
    base = CONV_HALO - (CONV_W - 1)
    y = ext_ref[pl.ds(base, tm), :] * wdw_ref[0:1, :]
    for k in range(1, CONV_W):
        y = y + ext_ref[pl.ds(base + k, tm), :] * wdw_ref[k:k + 1, :]
    b_ref[0] = _conv_tail(y, bdw_ref[...], lng_ref[...], lnb_ref[...]).astype(BF16)
    for h in range(N_HEADS_M):
        sl = slice(h * HEAD_DIM, (h + 1) * HEAD_DIM)
        m_ref[0, :, sl] = _mem_attend(qm_ref[0, :, sl], mk_ref[0, :, sl].astype(BF16),
                                      mv_ref[0, :, sl].astype(BF16)).astype(BF16)


def _prompt_mix(os_, ls_, glu, qm, mk, mv, wdw, bdw, lng, lnb, tm):
    B, S, W = glu.shape
    M = mk.shape[1]
    row = lambda b, i: (b, i, 0)
    halo = lambda b, i: (b, jnp.maximum(i * (tm // CONV_HALO) - 1, 0), 0)
    per_b = lambda b, i: (b, 0, 0)
    const = lambda b, i: (0, 0)
    blk = pl.BlockSpec((1, tm, W), row)
    return pl.pallas_call(
        _prompt_mix_kernel,
        grid=(B, S // tm),
        in_specs=[blk] * 7 + [pl.BlockSpec((1, CONV_HALO, W), halo), blk,
                              pl.BlockSpec((1, M, W), per_b), pl.BlockSpec((1, M, W), per_b),
                              pl.BlockSpec((CONV_W, W), const), pl.BlockSpec((1, W), const),
                              pl.BlockSpec((1, W), const), pl.BlockSpec((1, W), const)],
        out_specs=[blk] * 3,
        out_shape=[jax.ShapeDtypeStruct((B, S, W), BF16)] * 3,
        scratch_shapes=[pltpu.VMEM((tm + CONV_HALO, W), F32)],
        compiler_params=_cparams(("parallel", "parallel")),
    )(*os_, *ls_, glu, glu, qm, mk, mv, wdw, bdw, lng, lnb)


SROWS = 8


def _sample_mix_kernel(q0_ref, q1_ref, q2_ref, kn_ref, vn_ref, glu_ref, qm_ref, wk_ref, wv_ref,
                       mk_ref, mv_ref, st_ref, wdw_ref, bdw_ref, lng_ref, lnb_ref,
                       a_ref, b_ref, m_ref, nst_ref):
    n_buf = wk_ref.shape[1] // HEADS_PER_GROUP
    n_mem = mk_ref.shape[1] // N_HEADS_M
    head = lambda ref, j, n: ref[0, pl.ds(j, n, stride=HEADS_PER_GROUP), :].astype(BF16)
    row = lax.broadcasted_iota(jnp.int32, (SROWS, n_buf), 0)
    dist = n_buf - lax.broadcasted_iota(jnp.int32, (SROWS, n_buf), 1)
    (w0, d0), (w1, d1), (w2, d2) = DIL_GROUPS
    dmask = jnp.where(row == 0, d0 - 1, jnp.where(row == 1, d1 - 1, d2 - 1))
    win = jnp.where(row == 0, w0, jnp.where(row == 1, w1, w2))
    valid = jnp.logical_and((dist & dmask) == 0, dist <= win)
    for j in range(HEADS_PER_GROUP):
        sl = slice(j * HEAD_DIM, (j + 1) * HEAD_DIM)
        q8 = jnp.where(row[:, :HEAD_DIM] == 0, q0_ref[0, :, sl],
                       jnp.where(row[:, :HEAD_DIM] == 1, q1_ref[0, :, sl], q2_ref[0, :, sl]))
        s = jnp.where(valid, _dot_nt(q8.astype(BF16), head(wk_ref, j, n_buf)) * SCALE, NEG)
        s_new = jnp.sum(q8 * kn_ref[0, :, sl], axis=-1, keepdims=True) * SCALE
        mx = jnp.maximum(jnp.max(s, axis=-1, keepdims=True), s_new)
        p = jnp.exp(s - mx)
        p_new = jnp.exp(s_new - mx)
        den = jnp.sum(p, axis=-1, keepdims=True) + p_new
        o = (_dot(p.astype(BF16), head(wv_ref, j, n_buf)) + p_new * vn_ref[0, :, sl]) / den
        lse = mx + jnp.log(den)
        lm = jnp.maximum(jnp.maximum(lse[0:1], lse[1:2]), lse[2:3])
        g0, g1, g2 = jnp.exp(lse[0:1] - lm), jnp.exp(lse[1:2] - lm), jnp.exp(lse[2:3] - lm)
        a_ref[0, :, sl] = (g0 * o[0:1] + g1 * o[1:2] + g2 * o[2:3]) / (g0 + g1 + g2)
        qm8 = jnp.broadcast_to(qm_ref[0, :, sl], (SROWS, HEAD_DIM)).astype(BF16)
        mo = _mem_attend(qm8, head(mk_ref, j, n_mem), head(mv_ref, j, n_mem))
        m_ref[0, :, sl] = mo[0:1]
    glu = glu_ref[0]
    y = jnp.sum(st_ref[0] * wdw_ref[0:CONV_W - 1, :], axis=0, keepdims=True) + glu * wdw_ref[CONV_W - 1:CONV_W, :]
    b_ref[0] = _conv_tail(y, bdw_ref[...], lng_ref[...], lnb_ref[...])
    nst_ref[0, 0:CONV_W - 2, :] = st_ref[0, 1:CONV_W - 1, :]
    nst_ref[0, CONV_W - 2:CONV_W - 1, :] = glu


def _sample_mix(qs, kn, vn, glu, qm, wk, wv, mk, mv, st, wdw, bdw, lng, lnb):
    N, W = st.shape[0], st.shape[2]
    one = pl.BlockSpec((1, 1, W), lambda b: (b, 0, 0))
    per = lambda r: pl.BlockSpec((1, r, W), lambda b: (b, 0, 0))
    cache = lambda t: pl.BlockSpec((1,) + t.shape[1:], lambda b: (b, 0, 0))
    const = lambda b: (0, 0)
    return pl.pallas_call(
        _sample_mix_kernel,
        grid=(N,),
        in_specs=[one] * 7 + [cache(wk), cache(wv), cache(mk), cache(mv), per(CONV_W - 1),
                              pl.BlockSpec((CONV_W, W), const), pl.BlockSpec((1, W), const),
                              pl.BlockSpec((1, W), const), pl.BlockSpec((1, W), const)],
        out_specs=[one, one, one, per(CONV_W - 1)],
        out_shape=[jax.ShapeDtypeStruct((N, 1, W), F32)] * 3 + [jax.ShapeDtypeStruct((N, CONV_W - 1, W), F32)],
        compiler_params=_cparams(("parallel",)),
    )(*qs, kn, vn, glu, qm, wk, wv, mk, mv, st, wdw, bdw, lng, lnb)


def _back_kernel(x_ref, a_ref, b_ref, m_ref, g1_ref, wg_ref, bg_ref, wa_ref, wb_ref, wm_ref, wo_ref, g2_ref,
                 x1_ref, h2_ref):
    x = x_ref[...]
    hb = _rms(x, g1_ref[...]).astype(BF16)
    merged = None
    for i, (t_ref, w_ref) in enumerate(((a_ref, wa_ref), (b_ref, wb_ref), (m_ref, wm_ref))):
        sl = slice(i * D_MODEL, (i + 1) * D_MODEL)
        gate = _sigmoid(_dot(hb, wg_ref[:, sl]) + bg_ref[:, sl])
        term = gate * _dot(t_ref[...], w_ref[...])
        merged = term if merged is None else merged + term
    x1 = x + _dot(merged.astype(BF16), wo_ref[...])
    x1_ref[...] = x1
    h2_ref[...] = _pack_rows(_rms(x1, g2_ref[...]).astype(BF16))


def _back(x, a, b, m, g1, wg, bg, wa, wb, wm, wo, g2, tm):
    N, D = x.shape
    W = a.shape[1]
    row = lambda i: (i, 0)
    const = lambda i: (0, 0)
    return pl.pallas_call(
        _back_kernel,
        grid=(N // tm,),
        in_specs=[pl.BlockSpec((tm, D), row)] + [pl.BlockSpec((tm, W), row)] * 3 +
                 [pl.BlockSpec((1, D), const), pl.BlockSpec((D, 3 * D), const), pl.BlockSpec((1, 3 * D), const),
                  pl.BlockSpec((W, D), const), pl.BlockSpec((W, D), const), pl.BlockSpec((W, D), const),
                  pl.BlockSpec((D, D), const), pl.BlockSpec((1, D), const)],
        out_specs=[pl.BlockSpec((tm, D), row), pl.BlockSpec((tm // 2, D), row)],
        out_shape=[jax.ShapeDtypeStruct((N, D), F32), jax.ShapeDtypeStruct((N // 2, D), jnp.int32)],
        compiler_params=_cparams(("parallel",)),
    )(x, a, b, m, g1, wg, bg, wa, wb, wm, wo, g2)


PEER_CAND_ROWS = 56


def _top_sorted(s, k):
    out = []
    for _ in range(k):
        m = jnp.max(s, axis=0, keepdims=True)
        out.append(m)
        s = jnp.where(s == m, NEG, s)
    return out


def _prefix_count(test, t):
    c8 = test(t[7])
    c4 = test(jnp.where(c8, t[11], t[3]))
    c2 = test(jnp.where(c8, jnp.where(c4, t[13], t[9]), jnp.where(c4, t[5], t[1])))
    c1 = test(jnp.where(c8, jnp.where(c4, jnp.where(c2, t[14], t[12]), jnp.where(c2, t[10], t[8])),
                        jnp.where(c4, jnp.where(c2, t[6], t[4]), jnp.where(c2, t[2], t[0]))))
    n = (jnp.where(c8, 8.0, 0.0) + jnp.where(c4, 4.0, 0.0)) + (jnp.where(c2, 2.0, 0.0) + jnp.where(c1, 1.0, 0.0))
    return jnp.where(test(t[15]), 16.0, n)


def _peer_route_kernel(h_ref, wq_ref, sk_ref, r2_ref, e2_ref, qd_ref, e1d_ref, cand_ref):
    hb = _unpack_rows(h_ref[...])
    pairs = [(p, q) for p in range(PEER_TOPK) for q in range(PEER_TOPK) if (p + 1) * (q + 1) <= PEER_TOPK]
    cand_ref[...] = jnp.full(cand_ref.shape, NEG, F32)
    for h in range(PEER_HEADS):
        r0 = h * 2 * N_KEYS
        q1 = _dot_nt(wq_ref[r0:r0 + N_KEYS, :], hb).astype(BF16)
        q2 = _dot_nt(wq_ref[r0 + N_KEYS:r0 + 2 * N_KEYS, :], hb).astype(BF16)
        s1 = _dot(sk_ref[0], q1)
        s2 = _dot(sk_ref[1], q2)
        t1 = _top_sorted(s1, PEER_TOPK)
        t2 = _top_sorted(s2, PEER_TOPK)
        for r, (p, q) in enumerate(pairs):
            cand_ref[r:r + 1, :] = t1[p] + t2[q]
        top = _top_sorted(cand_ref[...], PEER_TOPK)
        z = jnp.exp(top[0] - top[0])
        for k in range(1, PEER_TOPK):
            z = z + jnp.exp(top[k] - top[0])
        tau = top[PEER_TOPK - 1]
        r2_ref[h] = _pack_rows(_prefix_count(lambda v: v > s2, t2).astype(BF16))
        qd_ref[h] = _prefix_count(lambda v: s1 + v >= tau, t2)
        e1d_ref[h] = jnp.exp(s1 - t1[0])
        e2_ref[h] = _pack_rows((jnp.exp(s2 - t2[0]) * (0.5 / z)).astype(BF16))


def _peer_route(h2, wq_t, sk, t):
    N, D = 2 * h2.shape[0], h2.shape[1]
    blk = lambda r: pl.BlockSpec((PEER_HEADS, r, t), lambda i: (0, 0, i))
    sds = lambda r: jax.ShapeDtypeStruct((PEER_HEADS, r, N), jnp.int32 if r < N_KEYS else F32)
    return pl.pallas_call(
        _peer_route_kernel,
        grid=(N // t,),
        in_specs=[pl.BlockSpec((t // 2, D), lambda i: (i, 0)),
                  pl.BlockSpec(wq_t.shape, lambda i: (0, 0)),
                  pl.BlockSpec(sk.shape, lambda i: (0, 0, 0))],
        out_specs=[blk(N_KEYS // 2), blk(N_KEYS // 2), blk(N_KEYS), blk(N_KEYS)],
        out_shape=[sds(N_KEYS // 2), sds(N_KEYS // 2), sds(N_KEYS), sds(N_KEYS)],
        scratch_shapes=[pltpu.VMEM((PEER_CAND_ROWS, t), F32)],
        compiler_params=_cparams(("parallel",)),
    )(h2, wq_t, sk)


PEER_CI = 16
PEER_SUB = 4
PEER_LANES = 256
GELU_C = 0.7978845608028654
GELU_A = 0.044715


def _peer_dense_kernel(h_ref, x1_ref, u_ref, vt_ref, r2_ref, e2_ref, qd_ref, e1d_ref, y_ref, acc_ref):
    c = pl.program_id(1)
    t = x1_ref.shape[0]
    lanes = min(PEER_LANES, t)
    sub = PEER_SUB * N_KEYS
    n_sub = PEER_CI // PEER_SUB

    @pl.when(c == 0)
    def _():
        acc_ref[...] = jnp.zeros_like(acc_ref)

    def row_bf16(ref, h, i, ls):
        return jnp.broadcast_to(ref[h, pl.ds(i, 1), ls], (N_KEYS, lanes)).astype(BF16)

    zero = jnp.zeros((N_KEYS, lanes), BF16)
    subs = [(l0, q) for l0 in range(0, t, lanes) for q in range(n_sub)]

    def first_matmul(k):
        l0, q = subs[k]
        u = _unpack_rows(u_ref[q * sub // 2:(q + 1) * sub // 2, :])
        return _dot_nt(u, _unpack_rows(h_ref[l0 // 2:(l0 + lanes) // 2, :]))

    def gated(k, at):
        l0, q = subs[k]
        ls = slice(l0, l0 + lanes)
        rows = []
        for ii in range(PEER_SUB):
            i = c * PEER_CI + q * PEER_SUB + ii
            w = None
            for h in range(PEER_HEADS):
                sel = _unpack_rows(r2_ref[h, :, ls]) < row_bf16(qd_ref, h, i, ls)
                term = jnp.where(sel, _unpack_rows(e2_ref[h, :, ls]), zero) * row_bf16(e1d_ref, h, i, ls)
                w = term if w is None else w + term
            a = at[ii * N_KEYS:(ii + 1) * N_KEYS, :]
            inner = a * ((a * a) * (GELU_C * GELU_A) + GELU_C)
            rows.append(w * (a * jnp.tanh(inner) + a).astype(BF16))
        return jnp.concatenate(rows, axis=0)

    parts = {}

    def second_matmul(k, hk):
        l0, q = subs[k]
        d = _dot(_unpack_rows(vt_ref[:, q * sub:(q + 1) * sub]), hk)
        parts[l0] = d if q == 0 else parts[l0] + d
        if q == n_sub - 1:
            acc_ref[:, l0:l0 + lanes] += parts[l0]

    at_next = first_matmul(0)
    h_prev = None
    for k in range(len(subs)):
        at = at_next
        if k + 1 < len(subs):
            at_next = first_matmul(k + 1)
        if k > 0:
            second_matmul(k - 1, h_prev)
        h_prev = gated(k, at)
    second_matmul(len(subs) - 1, h_prev)

    @pl.when(c == pl.num_programs(1) - 1)
    def _():
        y_ref[...] = x1_ref[...] + acc_ref[...].T


def _peer_dense(h2, x1, u_p, vt_p, r2, e2, qd, e1d, t):
    N, D = x1.shape
    E = 2 * u_p.shape[0]
    ce = PEER_CI * N_KEYS
    tok = lambda i, c: (i, 0)
    rt = lambda r: pl.BlockSpec((PEER_HEADS, r, t), lambda i, c: (0, 0, i))
    return pl.pallas_call(
        _peer_dense_kernel,
        grid=(N // t, E // ce),
        in_specs=[pl.BlockSpec((t // 2, D), tok), pl.BlockSpec((t, D), tok),
                  pl.BlockSpec((ce // 2, D), lambda i, c: (c, 0)),
                  pl.BlockSpec((D // 2, ce), lambda i, c: (0, c)),
                  rt(N_KEYS // 2), rt(N_KEYS // 2), rt(N_KEYS), rt(N_KEYS)],
        out_specs=pl.BlockSpec((t, D), tok),
        out_shape=jax.ShapeDtypeStruct((N, D), F32),
        scratch_shapes=[pltpu.VMEM((D, t), F32)],
        compiler_params=_cparams(("parallel", "arbitrary")),
    )(h2, x1, u_p, vt_p, r2, e2, qd, e1d)


def _peer(h2, x1, wq_t, sk, u_p, vt_p, t):
    N = x1.shape[0]
    pad = -N % t
    if pad:
        h2 = jnp.pad(h2, ((0, pad // 2), (0, 0)))
        x1 = jnp.pad(x1, ((0, pad), (0, 0)))
    routed = _peer_route(h2, wq_t, sk, min(t, PEER_LANES))
    return _peer_dense(h2, x1, u_p, vt_p, *routed, t)[:N]


def kernel(x_prompt, x_sample, mem_prompt, cache_win_k, cache_win_v, cache_mem_k, cache_mem_v, state_conv, g_norm1, w_in, b_gate, qn_a, kn_a, qn_m, kn_m, g_mem, w_mem_kv, w_dw, b_dw, ln_g, ln_b, w_a_proj, w_b_proj, w_m_proj, w_o, g_norm2, w_pq, sub_keys, u_tab, v_tab):
    B, S, D = x_prompt.shape
    NS = x_sample.shape[0]
    assert cache_win_k.shape[0] == 1 and x_sample.shape[1] == 1
    n_win = min(DIL_GROUPS[-1][0], S)
    tm = min(512, S)
    tp = min(512, B * S)

    w_main = w_in[0, :, :IN_MAIN_W].astype(BF16)
    w_gate = w_in[0, :, IN_MAIN_W:].astype(BF16)
    row = lambda v: v.reshape(1, -1)
    g1, g2, bg = row(g_norm1[0]), row(g_norm2[0]), row(b_gate[0])
    qna, kna, qnm, knm = row(qn_a[0]), row(kn_a[0]), row(qn_m[0]), row(kn_m[0])
    bdw, lng, lnb = row(b_dw[0]), row(ln_g[0]), row(ln_b[0])
    wdw = w_dw[0]
    wa, wb, wm, wo = (w[0].astype(BF16) for w in (w_a_proj, w_b_proj, w_m_proj, w_o))
    wq_t = w_pq[0].T.astype(BF16)
    sk = sub_keys[0].astype(BF16)
    u_b = _pack_table(u_tab[0], transpose=False)
    vt_b = _pack_table(v_tab[0], transpose=True)

    q0, q1, q2, kaf, vaf, glu, qm = _front(x_prompt, jnp.arange(S), g1, w_main, qna, kna, qnm, tm)
    outs = [_band_group(q, kaf, vaf, dil, max(BAND_STEP_ROWS, dil * BAND_BLK))
            for q, (_, dil) in zip((q0, q1, q2), DIL_GROUPS)]
    mk, mv = _mem_kv(mem_prompt, row(g_mem[0]), w_mem_kv[0].astype(BF16), knm)
    a, b, m = _prompt_mix([o for o, _ in outs], [l for _, l in outs], glu, qm, mk, mv, wdw, bdw, lng, lnb, tm)
    flat = lambda t: t.reshape(B * S, -1)
    x1, h2 = _back(flat(x_prompt), flat(a), flat(b), flat(m), g1, w_gate, bg, wa, wb, wm, wo, g2, tm)
    y_prompt = _peer(h2, x1, wq_t, sk, u_b, vt_b, tp).reshape(B, S, D)

    xs = x_sample.reshape(1, NS, D)
    pos_s = jnp.full((NS,), PAST_LEN, jnp.int32)
    sq0, sq1, sq2, skaf, svaf, sglu, sqm = _front(xs, pos_s, g1, w_main, qna, kna, qnm, NS)
    tok = lambda t: t.reshape(NS, 1, -1).astype(F32)
    cw = lambda t: t.reshape(NS, -1, HEAD_DIM)
    sa, sb, sm, new_conv_s = _sample_mix(
        (tok(sq0), tok(sq1), tok(sq2)), tok(skaf), tok(svaf), tok(sglu), tok(sqm),
        cw(cache_win_k), cw(cache_win_v), cw(cache_mem_k), cw(cache_mem_v), state_conv[0],
        wdw, bdw, lng, lnb)
    mixed = lambda t: t.reshape(NS, -1).astype(BF16)
    sx1, sh2 = _back(x_sample.reshape(NS, D), mixed(sa), mixed(sb), mixed(sm),
                     g1, w_gate, bg, wa, wb, wm, wo, g2, NS)
    y_sample = _peer(sh2, sx1, wq_t, sk, u_b, vt_b, 128).reshape(NS, 1, D)

    heads = lambda t, n: t.reshape(1, t.shape[0], n, HEADS_PER_GROUP, HEAD_DIM)
    return (y_prompt, y_sample,
            heads(kaf[:, S - n_win:], n_win), heads(vaf[:, S - n_win:], n_win),
            heads(mk, N_MEM), heads(mv, N_MEM),
            glu[None, :, S - (CONV_W - 1):],
            heads(skaf.reshape(NS, 1, -1), 1), heads(svaf.reshape(NS, 1, -1), 1),
            new_conv_s[None])
```

```python
import functools

import jax
import jax.numpy as jnp
from jax import lax
from jax.experimental import pallas as pl
from jax.experimental.pallas import tpu as pltpu

D_MODEL = 1024
HEAD_DIM = 128
ROT_DIM = HEAD_DIM // 4
ROPE_THETA = 500000.0
DIL_GROUPS = ((128, 1), (512, 4), (2048, 16))
HEADS_PER_GROUP = 4
N_GROUPS = len(DIL_GROUPS)
BAND_BLK = 128
CONV_CH = 512
CONV_W = 31
N_HEADS_M = 4
N_MEM = 256
PEER_HEADS = 8
N_KEYS = 128
PEER_TOPK = 16
PAST_LEN = 16384
EPS = 1e-6

GROUP_W = HEADS_PER_GROUP * HEAD_DIM
QA_W = N_GROUPS * GROUP_W
IN_MAIN_W = QA_W + 2 * GROUP_W + 2 * CONV_CH + N_HEADS_M * HEAD_DIM
OFF_KA = QA_W
OFF_VA = OFF_KA + GROUP_W
OFF_U = OFF_VA + GROUP_W
OFF_QM = OFF_U + 2 * CONV_CH
SCALE = HEAD_DIM ** -0.5
NEG = -1e30
CONV_HALO = 32
BAND_UNIT_BATCH = 4
BAND_STEP_ROWS = 1024

VMEM_LIMIT = 56 * 1024 * 1024
BF16 = jnp.bfloat16
F32 = jnp.float32

NT_DIMS = (((1,), (1,)), ((), ()))


def _cparams(sem):
    return pltpu.CompilerParams(dimension_semantics=sem, vmem_limit_bytes=VMEM_LIMIT)


def _rms(x, g):
    return x * lax.rsqrt(jnp.mean(x * x, axis=-1, keepdims=True) + EPS) * g


def _sigmoid(x):
    return 1.0 / (1.0 + jnp.exp(-x))


def _dot(a, b):
    return jnp.dot(a, b, preferred_element_type=F32)


def _dot_nt(a, b):
    return lax.dot_general(a, b, NT_DIMS, preferred_element_type=F32)


def _pack_rows(x):
    return pltpu.bitcast(x, jnp.int32)


def _unpack_rows(x):
    return pltpu.bitcast(x, BF16)


def _pack_table_kernel(x_ref, o_ref, *, transpose):
    x = x_ref[...]
    o_ref[...] = _pack_rows((x.T if transpose else x).astype(BF16))


def _pack_table(x, transpose, rows=512):
    R, C = x.shape
    if transpose:
        out_shape, out_spec = (C // 2, R), pl.BlockSpec((C // 2, rows), lambda i: (0, i))
    else:
        out_shape, out_spec = (R // 2, C), pl.BlockSpec((rows // 2, C), lambda i: (i, 0))
    return pl.pallas_call(
        functools.partial(_pack_table_kernel, transpose=transpose),
        grid=(R // rows,),
        in_specs=[pl.BlockSpec((rows, C), lambda i: (i, 0))],
        out_specs=out_spec,
        out_shape=jax.ShapeDtypeStruct(out_shape, jnp.int32),
        compiler_params=_cparams(("parallel",)),
    )(x)


def _front_kernel(x_ref, g1_ref, w_ref, qna_ref, kna_ref, qnm_ref, rc_ref, rsa_ref, rsb_ref,
                  qa0_ref, qa1_ref, qa2_ref, kaf_ref, vaf_ref, glu_ref, qm_ref):
    x = x_ref[0]
    hb = _rms(x, g1_ref[...]).astype(BF16)
    rc, rsa, rsb = rc_ref[...], rsa_ref[...], rsb_ref[...]

    def rot(y):
        return y * rc + pltpu.roll(y, HEAD_DIM - ROT_DIM // 2, 1) * rsa + pltpu.roll(y, ROT_DIM // 2, 1) * rsb

    qna, kna, qnm = qna_ref[...], kna_ref[...], qnm_ref[...]
    for g, q_ref in enumerate((qa0_ref, qa1_ref, qa2_ref)):
        p = _dot(hb, w_ref[:, g * GROUP_W:(g + 1) * GROUP_W])
        for h in range(HEADS_PER_GROUP):
            sl = slice(h * HEAD_DIM, (h + 1) * HEAD_DIM)
            q_ref[0, :, sl] = rot(_rms(p[:, sl], qna))
    p = _dot(hb, w_ref[:, OFF_KA:OFF_KA + GROUP_W])
    for h in range(HEADS_PER_GROUP):
        sl = slice(h * HEAD_DIM, (h + 1) * HEAD_DIM)
        kaf_ref[0, :, sl] = rot(_rms(p[:, sl], kna))
    vaf_ref[0] = _dot(hb, w_ref[:, OFF_VA:OFF_VA + GROUP_W])
    p = _dot(hb, w_ref[:, OFF_U:OFF_U + 2 * CONV_CH])
    glu_ref[0] = p[:, :CONV_CH] * _sigmoid(p[:, CONV_CH:])
    p = _dot(hb, w_ref[:, OFF_QM:OFF_QM + N_HEADS_M * HEAD_DIM])
    for h in range(N_HEADS_M):
        sl = slice(h * HEAD_DIM, (h + 1) * HEAD_DIM)
        qm_ref[0, :, sl] = _rms(p[:, sl], qnm).astype(BF16)


def _rope_tables(pos):
    half = ROT_DIM // 2
    inv = jnp.float32(ROPE_THETA) ** (-jnp.arange(half, dtype=F32) / half)
    ang = pos.astype(F32)[:, None] * inv[None, :]
    cos, sin = jnp.cos(ang), jnp.sin(ang)
    n = pos.shape[0]
    rest = HEAD_DIM - ROT_DIM
    rc = jnp.concatenate([cos, cos, jnp.ones((n, rest), F32)], axis=1)
    rsa = jnp.concatenate([-sin, jnp.zeros((n, half + rest), F32)], axis=1)
    rsb = jnp.concatenate([jnp.zeros((n, half), F32), sin, jnp.zeros((n, rest), F32)], axis=1)
    return rc, rsa, rsb


def _front(x, pos, g1, w_main, qna, kna, qnm, tm):
    B, S, D = x.shape
    rc, rsa, rsb = _rope_tables(pos)
    row = lambda b, i: (b, i, 0)
    const = lambda b, i: (0, 0)
    tab = lambda b, i: (i, 0)
    blk = pl.BlockSpec((1, tm, GROUP_W), row)
    sds = lambda dt: jax.ShapeDtypeStruct((B, S, GROUP_W), dt)
    return pl.pallas_call(
        _front_kernel,
        grid=(B, S // tm),
        in_specs=[pl.BlockSpec((1, tm, D), row),
                  pl.BlockSpec((1, D), const),
                  pl.BlockSpec((D, IN_MAIN_W), const),
                  pl.BlockSpec((1, HEAD_DIM), const),
                  pl.BlockSpec((1, HEAD_DIM), const),
                  pl.BlockSpec((1, HEAD_DIM), const),
                  pl.BlockSpec((tm, HEAD_DIM), tab),
                  pl.BlockSpec((tm, HEAD_DIM), tab),
                  pl.BlockSpec((tm, HEAD_DIM), tab)],
        out_specs=[blk] * 7,
        out_shape=[sds(F32)] * 6 + [sds(BF16)],
        compiler_params=_cparams(("parallel", "parallel")),
    )(x, g1, w_main, qna, kna, qnm, rc, rsa, rsb)


def _band_kernel(q_ref, kc_ref, kp_ref, vc_ref, vp_ref, o_ref, lse_ref, *, dil):
    span = dil * BAND_BLK
    n_units = q_ref.shape[1] // span
    qi = lax.broadcasted_iota(jnp.int32, (BAND_BLK, BAND_BLK), 0)
    ki = lax.broadcasted_iota(jnp.int32, (BAND_BLK, BAND_BLK), 1)
    mask_cur = ki <= qi
    mask_prev = ki >= qi
    mask_first = ki >= qi + jnp.where(pl.program_id(1) > 0, 0, BAND_BLK)

    def rows(ref, r, u):
        return ref[0, pl.ds(r + u * span, BAND_BLK, stride=dil), :] if dil > 1 else ref[0, pl.ds(u * span, BAND_BLK), :]

    def put(ref, r, u, val):
        if dil > 1:
            ref[0, pl.ds(r + u * span, BAND_BLK, stride=dil), :] = val
        else:
            ref[0, pl.ds(u * span, BAND_BLK), :] = val

    units = [(r, u) for r in range(dil) for u in range(n_units)]
    for b0 in range(0, len(units), BAND_UNIT_BATCH):
        batch = units[b0:b0 + BAND_UNIT_BATCH]
        scores, values = [], []
        for r, u in batch:
            q = rows(q_ref, r, u).astype(BF16)
            kc = rows(kc_ref, r, u).astype(BF16)
            if u > 0:
                kp, vp, mp = rows(kc_ref, r, u - 1), rows(vc_ref, r, u - 1), mask_prev
            else:
                kp, vp, mp = rows(kp_ref, r, 0), rows(vp_ref, r, 0), mask_first
            scores.append((jnp.where(mask_cur, _dot_nt(q, kc) * SCALE, NEG),
                           jnp.where(mp, _dot_nt(q, kp.astype(BF16)) * SCALE, NEG)))
            values.append((rows(vc_ref, r, u).astype(BF16), vp.astype(BF16)))
        stats = []
        for sc, sp in scores:
            m = jnp.maximum(jnp.max(sc, axis=-1, keepdims=True), jnp.max(sp, axis=-1, keepdims=True))
            pc = jnp.exp(sc - m)
            pp = jnp.exp(sp - m)
            den = jnp.sum(pc, axis=-1, keepdims=True) + jnp.sum(pp, axis=-1, keepdims=True)
            stats.append((pc.astype(BF16), pp.astype(BF16), m, den))
        for (r, u), (pc, pp, m, den), (vc, vp) in zip(batch, stats, values):
            put(o_ref, r, u, (_dot(pc, vc) + _dot(pp, vp)) / den)
            put(lse_ref, r, u, jnp.broadcast_to(m + jnp.log(den), (BAND_BLK, HEAD_DIM)))


def _band_group(q, k, v, dil, rows_per_step):
    B, S, W = q.shape
    span = dil * BAND_BLK
    rb = min(rows_per_step, S)
    assert rb % span == 0 and S % rb == 0
    cur = lambda b, i, h: (b, i, h)
    prev = lambda b, i, h: (b, jnp.maximum(i * (rb // span) - 1, 0), h)
    blk = pl.BlockSpec((1, rb, HEAD_DIM), cur)
    pblk = pl.BlockSpec((1, span, HEAD_DIM), prev)
    return pl.pallas_call(
        functools.partial(_band_kernel, dil=dil),
        grid=(B, S // rb, W // HEAD_DIM),
        in_specs=[blk, blk, pblk, blk, pblk],
        out_specs=[blk, blk],
        out_shape=[jax.ShapeDtypeStruct((B, S, W), F32)] * 2,
        compiler_params=_cparams(("parallel", "parallel", "parallel")),
    )(q, k, k, v, v)


def _mem_kv_kernel(mem_ref, g_ref, w_ref, kn_ref, mk_ref, mv_ref):
    hb = _rms(mem_ref[0], g_ref[...]).astype(BF16)
    kv = _dot(hb, w_ref[...])
    kn = kn_ref[...]
    width = N_HEADS_M * HEAD_DIM
    for h in range(N_HEADS_M):
        sl = slice(h * HEAD_DIM, (h + 1) * HEAD_DIM)
        mk_ref[0, :, sl] = _rms(kv[:, sl], kn)
    mv_ref[0] = kv[:, width:]


def _mem_kv(mem, g_mem, w_mem_kv, kn_m):
    B, M, D = mem.shape
    width = N_HEADS_M * HEAD_DIM
    const = lambda b: (0, 0)
    return pl.pallas_call(
        _mem_kv_kernel,
        grid=(B,),
        in_specs=[pl.BlockSpec((1, M, D), lambda b: (b, 0, 0)),
                  pl.BlockSpec((1, D), const),
                  pl.BlockSpec((D, 2 * width), const),
                  pl.BlockSpec((1, HEAD_DIM), const)],
        out_specs=[pl.BlockSpec((1, M, width), lambda b: (b, 0, 0))] * 2,
        out_shape=[jax.ShapeDtypeStruct((B, M, width), F32)] * 2,
        compiler_params=_cparams(("parallel",)),
    )(mem, g_mem, w_mem_kv, kn_m)


def _conv_tail(y, bdw, lng, lnb):
    yf = y + bdw
    mu = jnp.mean(yf, axis=-1, keepdims=True)
    c = yf - mu
    var = jnp.mean(c * c, axis=-1, keepdims=True)
    yn = c * lax.rsqrt(var + EPS) * lng + lnb
    return yn * _sigmoid(yn)


def _mem_attend(q, mk, mv):
    s = _dot_nt(q, mk) * SCALE
    p = jnp.exp(s - jnp.max(s, axis=-1, keepdims=True))
    return _dot(p.astype(BF16), mv) / jnp.sum(p, axis=-1, keepdims=True)


def _prompt_mix_kernel(o0_ref, o1_ref, o2_ref, l0_ref, l1_ref, l2_ref, glu_ref, halo_ref, qm_ref,
                       mk_ref, mv_ref, wdw_ref, bdw_ref, lng_ref, lnb_ref,
                       a_ref, b_ref, m_ref, ext_ref):
    tm = glu_ref.shape[1]
    l0, l1, l2 = l0_ref[0], l1_ref[0], l2_ref[0]
    lm = jnp.maximum(jnp.maximum(l0, l1), l2)
    w0, w1, w2 = jnp.exp(l0 - lm), jnp.exp(l1 - lm), jnp.exp(l2 - lm)
    a = (w0 * o0_ref[0] + w1 * o1_ref[0] + w2 * o2_ref[0]) / (w0 + w1 + w2)
    a_ref[0] = a.astype(BF16)
    keep = jnp.where(pl.program_id(1) == 0, 0.0, 1.0)
    ext_ref[0:CONV_HALO, :] = halo_ref[0] * keep
    ext_ref[CONV_HALO:, :] = glu_ref[0]
    base = CONV_HALO - (CONV_W - 1)
    y = ext_ref[pl.ds(base, tm), :] * wdw_ref[0:1, :]
    for k in range(1, CONV_W):
        y = y + ext_ref[pl.ds(base + k, tm), :] * wdw_ref[k:k + 1, :]
    b_ref[0] = _conv_tail(y, bdw_ref[...], lng_ref[...], lnb_ref[...]).astype(BF16)
    for h in range(N_HEADS_M):
        sl = slice(h * HEAD_DIM, (h + 1) * HEAD_DIM)
        m_ref[0, :, sl] = _mem_attend(qm_ref[0, :, sl], mk_ref[0, :, sl].astype(BF16),
                                      mv_ref[0, :, sl].astype(BF16)).astype(BF16)


def _prompt_mix(os_, ls_, glu, qm, mk, mv, wdw, bdw, lng, lnb, tm):
    B, S, W = glu.shape
    M = mk.shape[1]
    row = lambda b, i: (b, i, 0)
    halo = lambda b, i: (b, jnp.maximum(i * (tm // CONV_HALO) - 1, 0), 0)
    per_b = lambda b, i: (b, 0, 0)
    const = lambda b, i: (0, 0)
    blk = pl.BlockSpec((1, tm, W), row)
    return pl.pallas_call(
        _prompt_mix_kernel,
        grid=(B, S // tm),
        in_specs=[blk] * 7 + [pl.BlockSpec((1, CONV_HALO, W), halo), blk,
                              pl.BlockSpec((1, M, W), per_b), pl.BlockSpec((1, M, W), per_b),
                              pl.BlockSpec((CONV_W, W), const), pl.BlockSpec((1, W), const),
                              pl.BlockSpec((1, W), const), pl.BlockSpec((1, W), const)],
        out_specs=[blk] * 3,
        out_shape=[jax.ShapeDtypeStruct((B, S, W), BF16)] * 3,
        scratch_shapes=[pltpu.VMEM((tm + CONV_HALO, W), F32)],
        compiler_params=_cparams(("parallel", "parallel")),
    )(*os_, *ls_, glu, glu, qm, mk, mv, wdw, bdw, lng, lnb)


SROWS = 8


def _sample_mix_kernel(q0_ref, q1_ref, q2_ref, kn_ref, vn_ref, glu_ref, qm_ref, wk_ref, wv_ref,
                       mk_ref, mv_ref, st_ref, wdw_ref, bdw_ref, lng_ref, lnb_ref,
                       a_ref, b_ref, m_ref, nst_ref):
    n_buf = wk_ref.shape[1] // HEADS_PER_GROUP
    n_mem = mk_ref.shape[1] // N_HEADS_M
    head = lambda ref, j, n: ref[0, pl.ds(j, n, stride=HEADS_PER_GROUP), :].astype(BF16)
    row = lax.broadcasted_iota(jnp.int32, (SROWS, n_buf), 0)
    dist = n_buf - lax.broadcasted_iota(jnp.int32, (SROWS, n_buf), 1)
    (w0, d0), (w1, d1), (w2, d2) = DIL_GROUPS
    dmask = jnp.where(row == 0, d0 - 1, jnp.where(row == 1, d1 - 1, d2 - 1))
    win = jnp.where(row == 0, w0, jnp.where(row == 1, w1, w2))
    valid = jnp.logical_and((dist & dmask) == 0, dist <= win)
    for j in range(HEADS_PER_GROUP):
        sl = slice(j * HEAD_DIM, (j + 1) * HEAD_DIM)
        q8 = jnp.where(row[:, :HEAD_DIM] == 0, q0_ref[0, :, sl],
                       jnp.where(row[:, :HEAD_DIM] == 1, q1_ref[0, :, sl], q2_ref[0, :, sl]))
        s = jnp.where(valid, _dot_nt(q8.astype(BF16), head(wk_ref, j, n_buf)) * SCALE, NEG)
        s_new = jnp.sum(q8 * kn_ref[0, :, sl], axis=-1, keepdims=True) * SCALE
        mx = jnp.maximum(jnp.max(s, axis=-1, keepdims=True), s_new)
        p = jnp.exp(s - mx)
        p_new = jnp.exp(s_new - mx)
        den = jnp.sum(p, axis=-1, keepdims=True) + p_new
        o = (_dot(p.astype(BF16), head(wv_ref, j, n_buf)) + p_new * vn_ref[0, :, sl]) / den
        lse = mx + jnp.log(den)
        lm = jnp.maximum(jnp.maximum(lse[0:1], lse[1:2]), lse[2:3])
        g0, g1, g2 = jnp.exp(lse[0:1] - lm), jnp.exp(lse[1:2] - lm), jnp.exp(lse[2:3] - lm)
        a_ref[0, :, sl] = (g0 * o[0:1] + g1 * o[1:2] + g2 * o[2:3]) / (g0 + g1 + g2)
        qm8 = jnp.broadcast_to(qm_ref[0, :, sl], (SROWS, HEAD_DIM)).astype(BF16)
        mo = _mem_attend(qm8, head(mk_ref, j, n_mem), head(mv_ref, j, n_mem))
        m_ref[0, :, sl] = mo[0:1]
    glu = glu_ref[0]
    y = jnp.sum(st_ref[0] * wdw_ref[0:CONV_W - 1, :], axis=0, keepdims=True) + glu * wdw_ref[CONV_W - 1:CONV_W, :]
    b_ref[0] = _conv_tail(y, bdw_ref[...], lng_ref[...], lnb_ref[...])
    nst_ref[0, 0:CONV_W - 2, :] = st_ref[0, 1:CONV_W - 1, :]
    nst_ref[0, CONV_W - 2:CONV_W - 1, :] = glu


def _sample_mix(qs, kn, vn, glu, qm, wk, wv, mk, mv, st, wdw, bdw, lng, lnb):
    N, W = st.shape[0], st.shape[2]
    one = pl.BlockSpec((1, 1, W), lambda b: (b, 0, 0))
    per = lambda r: pl.BlockSpec((1, r, W), lambda b: (b, 0, 0))
    cache = lambda t: pl.BlockSpec((1,) + t.shape[1:], lambda b: (b, 0, 0))
    const = lambda b: (0, 0)
    return pl.pallas_call(
        _sample_mix_kernel,
        grid=(N,),
        in_specs=[one] * 7 + [cache(wk), cache(wv), cache(mk), cache(mv), per(CONV_W - 1),
                              pl.BlockSpec((CONV_W, W), const), pl.BlockSpec((1, W), const),
                              pl.BlockSpec((1, W), const), pl.BlockSpec((1, W), const)],
        out_specs=[one, one, one, per(CONV_W - 1)],
        out_shape=[jax.ShapeDtypeStruct((N, 1, W), F32)] * 3 + [jax.ShapeDtypeStruct((N, CONV_W - 1, W), F32)],
        compiler_params=_cparams(("parallel",)),
    )(*qs, kn, vn, glu, qm, wk, wv, mk, mv, st, wdw, bdw, lng, lnb)


def _back_kernel(x_ref, a_ref, b_ref, m_ref, g1_ref, wg_ref, bg_ref, wa_ref, wb_ref, wm_ref, wo_ref, g2_ref,
                 x1_ref, h2_ref):
    x = x_ref[...]
    hb = _rms(x, g1_ref[...]).astype(BF16)
    merged = None
    for i, (t_ref, w_ref) in enumerate(((a_ref, wa_ref), (b_ref, wb_ref), (m_ref, wm_ref))):
        sl = slice(i * D_MODEL, (i + 1) * D_MODEL)
        gate = _sigmoid(_dot(hb, wg_ref[:, sl]) + bg_ref[:, sl])
        term = gate * _dot(t_ref[...], w_ref[...])
        merged = term if merged is None else merged + term
    x1 = x + _dot(merged.astype(BF16), wo_ref[...])
    x1_ref[...] = x1
    h2_ref[...] = _pack_rows(_rms(x1, g2_ref[...]).astype(BF16))


def _back(x, a, b, m, g1, wg, bg, wa, wb, wm, wo, g2, tm):
    N, D = x.shape
    W = a.shape[1]
    row = lambda i: (i, 0)
    const = lambda i: (0, 0)
    return pl.pallas_call(
        _back_kernel,
        grid=(N // tm,),
        in_specs=[pl.BlockSpec((tm, D), row)] + [pl.BlockSpec((tm, W), row)] * 3 +
                 [pl.BlockSpec((1, D), const), pl.BlockSpec((D, 3 * D), const), pl.BlockSpec((1, 3 * D), const),
                  pl.BlockSpec((W, D), const), pl.BlockSpec((W, D), const), pl.BlockSpec((W, D), const),
                  pl.BlockSpec((D, D), const), pl.BlockSpec((1, D), const)],
        out_specs=[pl.BlockSpec((tm, D), row), pl.BlockSpec((tm // 2, D), row)],
        out_shape=[jax.ShapeDtypeStruct((N, D), F32), jax.ShapeDtypeStruct((N // 2, D), jnp.int32)],
        compiler_params=_cparams(("parallel",)),
    )(x, a, b, m, g1, wg, bg, wa, wb, wm, wo, g2)


PEER_CAND_ROWS = 56


def _top_sorted(s, k):
    out = []
    for _ in range(k):
        m = jnp.max(s, axis=0, keepdims=True)
        out.append(m)
        s = jnp.where(s == m, NEG, s)
    return out


def _prefix_count(test, t):
    c8 = test(t[7])
    c4 = test(jnp.where(c8, t[11], t[3]))
    c2 = test(jnp.where(c8, jnp.where(c4, t[13], t[9]), jnp.where(c4, t[5], t[1])))
    c1 = test(jnp.where(c8, jnp.where(c4, jnp.where(c2, t[14], t[12]), jnp.where(c2, t[10], t[8])),
                        jnp.where(c4, jnp.where(c2, t[6], t[4]), jnp.where(c2, t[2], t[0]))))
    n = (jnp.where(c8, 8.0, 0.0) + jnp.where(c4, 4.0, 0.0)) + (jnp.where(c2, 2.0, 0.0) + jnp.where(c1, 1.0, 0.0))
    return jnp.where(test(t[15]), 16.0, n)


def _peer_route_kernel(h_ref, wq_ref, sk_ref, r2_ref, e2_ref, qd_ref, e1d_ref, cand_ref):
    hb = _unpack_rows(h_ref[...])
    pairs = [(p, q) for p in range(PEER_TOPK) for q in range(PEER_TOPK) if (p + 1) * (q + 1) <= PEER_TOPK]
    cand_ref[...] = jnp.full(cand_ref.shape, NEG, F32)
    for h in range(PEER_HEADS):
        r0 = h * 2 * N_KEYS
        q1 = _dot_nt(wq_ref[r0:r0 + N_KEYS, :], hb).astype(BF16)
        q2 = _dot_nt(wq_ref[r0 + N_KEYS:r0 + 2 * N_KEYS, :], hb).astype(BF16)
        s1 = _dot(sk_ref[0], q1)
        s2 = _dot(sk_ref[1], q2)
        t1 = _top_sorted(s1, PEER_TOPK)
        t2 = _top_sorted(s2, PEER_TOPK)
        for r, (p, q) in enumerate(pairs):
            cand_ref[r:r + 1, :] = t1[p] + t2[q]
        top = _top_sorted(cand_ref[...], PEER_TOPK)
        z = jnp.exp(top[0] - top[0])
        for k in range(1, PEER_TOPK):
            z = z + jnp.exp(top[k] - top[0])
        tau = top[PEER_TOPK - 1]
        r2_ref[h] = _pack_rows(_prefix_count(lambda v: v > s2, t2).astype(BF16))
        qd_ref[h] = _prefix_count(lambda v: s1 + v >= tau, t2)
        e1d_ref[h] = jnp.exp(s1 - t1[0])
        e2_ref[h] = _pack_rows((jnp.exp(s2 - t2[0]) * (0.5 / z)).astype(BF16))


def _peer_route(h2, wq_t, sk, t):
    N, D = 2 * h2.shape[0], h2.shape[1]
    blk = lambda r: pl.BlockSpec((PEER_HEADS, r, t), lambda i: (0, 0, i))
    sds = lambda r: jax.ShapeDtypeStruct((PEER_HEADS, r, N), jnp.int32 if r < N_KEYS else F32)
    return pl.pallas_call(
        _peer_route_kernel,
        grid=(N // t,),
        in_specs=[pl.BlockSpec((t // 2, D), lambda i: (i, 0)),
                  pl.BlockSpec(wq_t.shape, lambda i: (0, 0)),
                  pl.BlockSpec(sk.shape, lambda i: (0, 0, 0))],
        out_specs=[blk(N_KEYS // 2), blk(N_KEYS // 2), blk(N_KEYS), blk(N_KEYS)],
        out_shape=[sds(N_KEYS // 2), sds(N_KEYS // 2), sds(N_KEYS), sds(N_KEYS)],
        scratch_shapes=[pltpu.VMEM((PEER_CAND_ROWS, t), F32)],
        compiler_params=_cparams(("parallel",)),
    )(h2, wq_t, sk)


PEER_CI = 16
PEER_SUB = 4
PEER_LANES = 256
PEER_W_LANES = 128
PEER_OUT_SPLIT = 2
GELU_C = 0.7978845608028654
GELU_A = 0.044715


def _peer_dense_kernel(h_ref, x1_ref, u_ref, vt_ref, r2_ref, e2_ref, qd_ref, e1d_ref, y_ref, acc_ref):
    c = pl.program_id(1)
    t = x1_ref.shape[0]
    lanes = min(PEER_LANES, t)
    sub = PEER_SUB * N_KEYS
    n_sub = PEER_CI // PEER_SUB

    @pl.when(c == 0)
    def _():
        acc_ref[...] = jnp.zeros_like(acc_ref)

    wl = min(PEER_W_LANES, lanes)

    def row_bf16(row, w0):
        return jnp.broadcast_to(row[:, w0:w0 + wl], (N_KEYS, wl)).astype(BF16)

    zero = jnp.zeros((N_KEYS, wl), BF16)
    subs = [(l0, q) for l0 in range(0, t, lanes) for q in range(n_sub)]

    def first_matmul(k):
        l0, q = subs[k]
        u = _unpack_rows(u_ref[q * sub // 2:(q + 1) * sub // 2, :])
        return _dot_nt(u, _unpack_rows(h_ref[l0 // 2:(l0 + lanes) // 2, :]))

    def gated(k, at):
        l0, q = subs[k]
        rows = []
        for ii in range(PEER_SUB):
            i = c * PEER_CI + q * PEER_SUB + ii
            counts = [qd_ref[h, pl.ds(i, 1), l0:l0 + lanes] for h in range(PEER_HEADS)]
            e1s = [e1d_ref[h, pl.ds(i, 1), l0:l0 + lanes] for h in range(PEER_HEADS)]
            cols = []
            for w0 in range(0, lanes, wl):
                ls = slice(l0 + w0, l0 + w0 + wl)
                w = None
                for h in range(PEER_HEADS):
                    sel = _unpack_rows(r2_ref[h, :, ls]) < row_bf16(counts[h], w0)
                    term = jnp.where(sel, _unpack_rows(e2_ref[h, :, ls]), zero) * row_bf16(e1s[h], w0)
                    w = term if w is None else w + term
                a = at[ii * N_KEYS:(ii + 1) * N_KEYS, w0:w0 + wl]
                inner = a * ((a * a) * (GELU_C * GELU_A) + GELU_C)
                cols.append(w * (a * jnp.tanh(inner) + a).astype(BF16))
            rows.append(cols[0] if len(cols) == 1 else jnp.concatenate(cols, axis=1))
        return jnp.concatenate(rows, axis=0)

    parts = {}

    d_model = acc_ref.shape[0]
    out_rows = d_model // PEER_OUT_SPLIT

    def second_matmul(k, hk):
        l0, q = subs[k]
        for o0 in range(0, d_model, out_rows):
            d = _dot(_unpack_rows(vt_ref[o0 // 2:(o0 + out_rows) // 2, q * sub:(q + 1) * sub]), hk)
            parts[l0, o0] = d if q == 0 else parts[l0, o0] + d
            if q == n_sub - 1:
                acc_ref[o0:o0 + out_rows, l0:l0 + lanes] += parts[l0, o0]

    at_next = first_matmul(0)
    h_prev = None
    for k in range(len(subs)):
        at = at_next
        if k + 1 < len(subs):
            at_next = first_matmul(k + 1)
        if k > 0:
            second_matmul(k - 1, h_prev)
        h_prev = gated(k, at)
    second_matmul(len(subs) - 1, h_prev)

    @pl.when(c == pl.num_programs(1) - 1)
    def _():
        y_ref[...] = x1_ref[...] + acc_ref[...].T


def _peer_dense(h2, x1, u_p, vt_p, r2, e2, qd, e1d, t):
    N, D = x1.shape
    E = 2 * u_p.shape[0]
    ce = PEER_CI * N_KEYS
    tok = lambda i, c: (i, 0)
    rt = lambda r: pl.BlockSpec((PEER_HEADS, r, t), lambda i, c: (0, 0, i))
    return pl.pallas_call(
        _peer_dense_kernel,
        grid=(N // t, E // ce),
        in_specs=[pl.BlockSpec((t // 2, D), tok), pl.BlockSpec((t, D), tok),
                  pl.BlockSpec((ce // 2, D), lambda i, c: (c, 0)),
                  pl.BlockSpec((D // 2, ce), lambda i, c: (0, c)),
                  rt(N_KEYS // 2), rt(N_KEYS // 2), rt(N_KEYS), rt(N_KEYS)],
        out_specs=pl.BlockSpec((t, D), tok),
        out_shape=jax.ShapeDtypeStruct((N, D), F32),
        scratch_shapes=[pltpu.VMEM((D, t), F32)],
        compiler_params=_cparams(("parallel", "arbitrary")),
    )(h2, x1, u_p, vt_p, r2, e2, qd, e1d)


def _peer(h2, x1, wq_t, sk, u_p, vt_p, t):
    N = x1.shape[0]
    pad = -N % t
    if pad:
        h2 = jnp.pad(h2, ((0, pad // 2), (0, 0)))
        x1 = jnp.pad(x1, ((0, pad), (0, 0)))
    routed = _peer_route(h2, wq_t, sk, min(t, PEER_LANES))
    return _peer_dense(h2, x1, u_p, vt_p, *routed, t)[:N]


def kernel(x_prompt, x_sample, mem_prompt, cache_win_k, cache_win_v, cache_mem_k, cache_mem_v, state_conv, g_norm1, w_in, b_gate, qn_a, kn_a, qn_m, kn_m, g_mem, w_mem_kv, w_dw, b_dw, ln_g, ln_b, w_a_proj, w_b_proj, w_m_proj, w_o, g_norm2, w_pq, sub_keys, u_tab, v_tab):
    B, S, D = x_prompt.shape
    NS = x_sample.shape[0]
    assert cache_win_k.shape[0] == 1 and x_sample.shape[1] == 1
    n_win = min(DIL_GROUPS[-1][0], S)
    tm = min(512, S)
    tp = min(512, B * S)

    w_main = w_in[0, :, :IN_MAIN_W].astype(BF16)
    w_gate = w_in[0, :, IN_MAIN_W:].astype(BF16)
    row = lambda v: v.reshape(1, -1)
    g1, g2, bg = row(g_norm1[0]), row(g_norm2[0]), row(b_gate[0])
    qna, kna, qnm, knm = row(qn_a[0]), row(kn_a[0]), row(qn_m[0]), row(kn_m[0])
    bdw, lng, lnb = row(b_dw[0]), row(ln_g[0]), row(ln_b[0])
    wdw = w_dw[0]
    wa, wb, wm, wo = (w[0].astype(BF16) for w in (w_a_proj, w_b_proj, w_m_proj, w_o))
    wq_t = w_pq[0].T.astype(BF16)
    sk = sub_keys[0].astype(BF16)
    u_b = _pack_table(u_tab[0], transpose=False)
    vt_b = _pack_table(v_tab[0], transpose=True)

    q0, q1, q2, kaf, vaf, glu, qm = _front(x_prompt, jnp.arange(S), g1, w_main, qna, kna, qnm, tm)
    outs = [_band_group(q, kaf, vaf, dil, max(BAND_STEP_ROWS, dil * BAND_BLK))
            for q, (_, dil) in zip((q0, q1, q2), DIL_GROUPS)]
    mk, mv = _mem_kv(mem_prompt, row(g_mem[0]), w_mem_kv[0].astype(BF16), knm)
    a, b, m = _prompt_mix([o for o, _ in outs], [l for _, l in outs], glu, qm, mk, mv, wdw, bdw, lng, lnb, tm)
    flat = lambda t: t.reshape(B * S, -1)
    x1, h2 = _back(flat(x_prompt), flat(a), flat(b), flat(m), g1, w_gate, bg, wa, wb, wm, wo, g2, tm)
    y_prompt = _peer(h2, x1, wq_t, sk, u_b, vt_b, tp).reshape(B, S, D)

    xs = x_sample.reshape(1, NS, D)
    pos_s = jnp.full((NS,), PAST_LEN, jnp.int32)
    sq0, sq1, sq2, skaf, svaf, sglu, sqm = _front(xs, pos_s, g1, w_main, qna, kna, qnm, NS)
    tok = lambda t: t.reshape(NS, 1, -1).astype(F32)
    cw = lambda t: t.reshape(NS, -1, HEAD_DIM)
    sa, sb, sm, new_conv_s = _sample_mix(
        (tok(sq0), tok(sq1), tok(sq2)), tok(skaf), tok(svaf), tok(sglu), tok(sqm),
        cw(cache_win_k), cw(cache_win_v), cw(cache_mem_k), cw(cache_mem_v), state_conv[0],
        wdw, bdw, lng, lnb)
    mixed = lambda t: t.reshape(NS, -1).astype(BF16)
    sx1, sh2 = _back(x_sample.reshape(NS, D), mixed(sa), mixed(sb), mixed(sm),
                     g1, w_gate, bg, wa, wb, wm, wo, g2, NS)
    y_sample = _peer(sh2, sx1, wq_t, sk, u_b, vt_b, 128).reshape(NS, 1, D)

    heads = lambda t, n: t.reshape(1, t.shape[0], n, HEADS_PER_GROUP, HEAD_DIM)
    return (y_prompt, y_sample,
            heads(kaf[:, S - n_win:], n_win), heads(vaf[:, S - n_win:], n_win),
            heads(mk, N_MEM), heads(mv, N_MEM),
            glu[None, :, S - (CONV_W - 1):],
            heads(skaf.reshape(NS, 1, -1), 1), heads(svaf.reshape(NS, 1, -1), 1),
            new_conv_s[None])
```

```python
import functools

import jax
import jax.numpy as jnp
from jax import lax
from jax.experimental import pallas as pl
from jax.experimental.pallas import tpu as pltpu

D_MODEL = 1024
HEAD_DIM = 128
ROT_DIM = HEAD_DIM // 4
ROPE_THETA = 500000.0
DIL_GROUPS = ((128, 1), (512, 4), (2048, 16))
HEADS_PER_GROUP = 4
N_GROUPS = len(DIL_GROUPS)
BAND_BLK = 128
CONV_CH = 512
CONV_W = 31
N_HEADS_M = 4
N_MEM = 256
PEER_HEADS = 8
N_KEYS = 128
PEER_TOPK = 16
PAST_LEN = 16384
EPS = 1e-6

GROUP_W = HEADS_PER_GROUP * HEAD_DIM
QA_W = N_GROUPS * GROUP_W
IN_MAIN_W = QA_W + 2 * GROUP_W + 2 * CONV_CH + N_HEADS_M * HEAD_DIM
OFF_KA = QA_W
OFF_VA = OFF_KA + GROUP_W
OFF_U = OFF_VA + GROUP_W
OFF_QM = OFF_U + 2 * CONV_CH
SCALE = HEAD_DIM ** -0.5
NEG = -1e30
SUBLANES = 8
CONV_HALO = 32
BAND_UNIT_BATCH = 4
BAND_STEP_ROWS = 2048

VMEM_LIMIT = 56 * 1024 * 1024
BF16 = jnp.bfloat16
F32 = jnp.float32

NT_DIMS = (((1,), (1,)), ((), ()))


def _cparams(sem):
    return pltpu.CompilerParams(dimension_semantics=sem, vmem_limit_bytes=VMEM_LIMIT)


def _rms(x, g):
    return x * lax.rsqrt(jnp.mean(x * x, axis=-1, keepdims=True) + EPS) * g


def _sigmoid(x):
    return 1.0 / (1.0 + jnp.exp(-x))


def _dot(a, b):
    return jnp.dot(a, b, preferred_element_type=F32)


def _dot_nt(a, b):
    return lax.dot_general(a, b, NT_DIMS, preferred_element_type=F32)


def _pack_rows(x):
    return pltpu.bitcast(x, jnp.int32)


def _unpack_rows(x):
    return pltpu.bitcast(x, BF16)


def _pack_table_kernel(x_ref, o_ref, *, transpose):
    x = x_ref[...]
    o_ref[...] = _pack_rows((x.T if transpose else x).astype(BF16))


def _pack_table(x, transpose, rows=512):
    R, C = x.shape
    if transpose:
        out_shape, out_spec = (C // 2, R), pl.BlockSpec((C // 2, rows), lambda i: (0, i))
    else:
        out_shape, out_spec = (R // 2, C), pl.BlockSpec((rows // 2, C), lambda i: (i, 0))
    return pl.pallas_call(
        functools.partial(_pack_table_kernel, transpose=transpose),
        grid=(R // rows,),
        in_specs=[pl.BlockSpec((rows, C), lambda i: (i, 0))],
        out_specs=out_spec,
        out_shape=jax.ShapeDtypeStruct(out_shape, jnp.int32),
        compiler_params=_cparams(("parallel",)),
    )(x)


def _front_kernel(x_ref, g1_ref, w_ref, qna_ref, kna_ref, qnm_ref, rc_ref, rsa_ref, rsb_ref,
                  qa0_ref, qa1_ref, qa2_ref, kaf_ref, vaf_ref, glu_ref, qm_ref):
    x = x_ref[0]
    hb = _rms(x, g1_ref[...]).astype(BF16)
    rc, rsa, rsb = rc_ref[...], rsa_ref[...], rsb_ref[...]

    def rot(y):
        return y * rc + pltpu.roll(y, HEAD_DIM - ROT_DIM // 2, 1) * rsa + pltpu.roll(y, ROT_DIM // 2, 1) * rsb

    qna, kna, qnm = qna_ref[...], kna_ref[...], qnm_ref[...]
    for g, q_ref in enumerate((qa0_ref, qa1_ref, qa2_ref)):
        p = _dot(hb, w_ref[:, g * GROUP_W:(g + 1) * GROUP_W])
        for h in range(HEADS_PER_GROUP):
            sl = slice(h * HEAD_DIM, (h + 1) * HEAD_DIM)
            q_ref[0, :, sl] = rot(_rms(p[:, sl], qna))
    p = _dot(hb, w_ref[:, OFF_KA:OFF_KA + GROUP_W])
    for h in range(HEADS_PER_GROUP):
        sl = slice(h * HEAD_DIM, (h + 1) * HEAD_DIM)
        kaf_ref[0, :, sl] = rot(_rms(p[:, sl], kna))
    vaf_ref[0] = _dot(hb, w_ref[:, OFF_VA:OFF_VA + GROUP_W])
    p = _dot(hb, w_ref[:, OFF_U:OFF_U + 2 * CONV_CH])
    glu_ref[0] = p[:, :CONV_CH] * _sigmoid(p[:, CONV_CH:])
    p = _dot(hb, w_ref[:, OFF_QM:OFF_QM + N_HEADS_M * HEAD_DIM])
    for h in range(N_HEADS_M):
        sl = slice(h * HEAD_DIM, (h + 1) * HEAD_DIM)
        qm_ref[0, :, sl] = _rms(p[:, sl], qnm).astype(BF16)


def _rope_tables(pos):
    half = ROT_DIM // 2
    inv = jnp.float32(ROPE_THETA) ** (-jnp.arange(half, dtype=F32) / half)
    ang = pos.astype(F32)[:, None] * inv[None, :]
    cos, sin = jnp.cos(ang), jnp.sin(ang)
    n = pos.shape[0]
    rest = HEAD_DIM - ROT_DIM
    rc = jnp.concatenate([cos, cos, jnp.ones((n, rest), F32)], axis=1)
    rsa = jnp.concatenate([-sin, jnp.zeros((n, half + rest), F32)], axis=1)
    rsb = jnp.concatenate([jnp.zeros((n, half), F32), sin, jnp.zeros((n, rest), F32)], axis=1)
    return rc, rsa, rsb


def _front(x, pos, g1, w_main, qna, kna, qnm, tm):
    B, S, D = x.shape
    rc, rsa, rsb = _rope_tables(pos)
    row = lambda b, i: (b, i, 0)
    const = lambda b, i: (0, 0)
    tab = lambda b, i: (i, 0)
    blk = pl.BlockSpec((1, tm, GROUP_W), row)
    sds = lambda dt: jax.ShapeDtypeStruct((B, S, GROUP_W), dt)
    return pl.pallas_call(
        _front_kernel,
        grid=(B, S // tm),
        in_specs=[pl.BlockSpec((1, tm, D), row),
                  pl.BlockSpec((1, D), const),
                  pl.BlockSpec((D, IN_MAIN_W), const),
                  pl.BlockSpec((1, HEAD_DIM), const),
                  pl.BlockSpec((1, HEAD_DIM), const),
                  pl.BlockSpec((1, HEAD_DIM), const),
                  pl.BlockSpec((tm, HEAD_DIM), tab),
                  pl.BlockSpec((tm, HEAD_DIM), tab),
                  pl.BlockSpec((tm, HEAD_DIM), tab)],
        out_specs=[blk] * 7,
        out_shape=[sds(F32)] * 6 + [sds(BF16)],
        compiler_params=_cparams(("parallel", "parallel")),
    )(x, g1, w_main, qna, kna, qnm, rc, rsa, rsb)


def _band_kernel(q_ref, kc_ref, kp_ref, vc_ref, vp_ref, o_ref, lse_ref, *, dil):
    span = dil * BAND_BLK
    n_units = q_ref.shape[1] // span
    qi = lax.broadcasted_iota(jnp.int32, (BAND_BLK, BAND_BLK), 0)
    ki = lax.broadcasted_iota(jnp.int32, (BAND_BLK, BAND_BLK), 1)
    mask_cur = ki <= qi
    mask_prev = ki >= qi
    mask_first = ki >= qi + jnp.where(pl.program_id(1) > 0, 0, BAND_BLK)

    def rows(ref, r, u):
        return ref[0, pl.ds(r + u * span, BAND_BLK, stride=dil), :] if dil > 1 else ref[0, pl.ds(u * span, BAND_BLK), :]

    def put(ref, r, u, val):
        if dil > 1:
            ref[0, pl.ds(r + u * span, BAND_BLK, stride=dil), :] = val
        else:
            ref[0, pl.ds(u * span, BAND_BLK), :] = val

    units = [(r, u) for r in range(dil) for u in range(n_units)]
    for b0 in range(0, len(units), BAND_UNIT_BATCH):
        batch = units[b0:b0 + BAND_UNIT_BATCH]
        scores, values = [], []
        for r, u in batch:
            q = rows(q_ref, r, u).astype(BF16)
            kc = rows(kc_ref, r, u).astype(BF16)
            if u > 0:
                kp, vp, mp = rows(kc_ref, r, u - 1), rows(vc_ref, r, u - 1), mask_prev
            else:
                kp, vp, mp = rows(kp_ref, r, 0), rows(vp_ref, r, 0), mask_first
            scores.append((jnp.where(mask_cur, _dot_nt(q, kc) * SCALE, NEG),
                           jnp.where(mp, _dot_nt(q, kp.astype(BF16)) * SCALE, NEG)))
            values.append((rows(vc_ref, r, u).astype(BF16), vp.astype(BF16)))
        stats = []
        for sc, sp in scores:
            m = jnp.maximum(jnp.max(sc, axis=-1, keepdims=True), jnp.max(sp, axis=-1, keepdims=True))
            pc = jnp.exp(sc - m)
            pp = jnp.exp(sp - m)
            den = jnp.sum(pc, axis=-1, keepdims=True) + jnp.sum(pp, axis=-1, keepdims=True)
            stats.append((pc.astype(BF16), pp.astype(BF16), m, den))
        for (r, u), (pc, pp, m, den), (vc, vp) in zip(batch, stats, values):
            put(o_ref, r, u, (_dot(pc, vc) + _dot(pp, vp)) / den)
            put(lse_ref, r, u, jnp.broadcast_to(m + jnp.log(den), (BAND_BLK, HEAD_DIM)))


def _band_group(q, k, v, dil, rows_per_step):
    B, S, W = q.shape
    span = dil * BAND_BLK
    rb = min(rows_per_step, S)
    assert rb % span == 0 and S % rb == 0
    cur = lambda b, i, h: (b, i, h)
    prev = lambda b, i, h: (b, jnp.maximum(i * (rb // span) - 1, 0), h)
    blk = pl.BlockSpec((1, rb, HEAD_DIM), cur)
    pblk = pl.BlockSpec((1, span, HEAD_DIM), prev)
    return pl.pallas_call(
        functools.partial(_band_kernel, dil=dil),
        grid=(B, S // rb, W // HEAD_DIM),
        in_specs=[blk, blk, pblk, blk, pblk],
        out_specs=[blk, blk],
        out_shape=[jax.ShapeDtypeStruct((B, S, W), F32)] * 2,
        compiler_params=_cparams(("parallel", "parallel", "parallel")),
    )(q, k, k, v, v)


def _mem_kv_kernel(mem_ref, g_ref, w_ref, kn_ref, mk_ref, mv_ref):
    hb = _rms(mem_ref[0], g_ref[...]).astype(BF16)
    kv = _dot(hb, w_ref[...])
    kn = kn_ref[...]
    width = N_HEADS_M * HEAD_DIM
    for h in range(N_HEADS_M):
        sl = slice(h * HEAD_DIM, (h + 1) * HEAD_DIM)
        mk_ref[0, :, sl] = _rms(kv[:, sl], kn)
    mv_ref[0] = kv[:, width:]


def _mem_kv(mem, g_mem, w_mem_kv, kn_m):
    B, M, D = mem.shape
    width = N_HEADS_M * HEAD_DIM
    const = lambda b: (0, 0)
    return pl.pallas_call(
        _mem_kv_kernel,
        grid=(B,),
        in_specs=[pl.BlockSpec((1, M, D), lambda b: (b, 0, 0)),
                  pl.BlockSpec((1, D), const),
                  pl.BlockSpec((D, 2 * width), const),
                  pl.BlockSpec((1, HEAD_DIM), const)],
        out_specs=[pl.BlockSpec((1, M, width), lambda b: (b, 0, 0))] * 2,
        out_shape=[jax.ShapeDtypeStruct((B, M, width), F32)] * 2,
        compiler_params=_cparams(("parallel",)),
    )(mem, g_mem, w_mem_kv, kn_m)


def _conv_tail(y, bdw, lng, lnb):
    yf = y + bdw
    mu = jnp.mean(yf, axis=-1, keepdims=True)
    c = yf - mu
    var = jnp.mean(c * c, axis=-1, keepdims=True)
    yn = c * lax.rsqrt(var + EPS) * lng + lnb
    return yn * _sigmoid(yn)


def _mem_attend(q, mk, mv):
    s = _dot_nt(q, mk) * SCALE
    p = jnp.exp(s - jnp.max(s, axis=-1, keepdims=True))
    return _dot(p.astype(BF16), mv) / jnp.sum(p, axis=-1, keepdims=True)


def _prompt_mix_kernel(o0_ref, o1_ref, o2_ref, l0_ref, l1_ref, l2_ref, glu_ref, halo_ref, qm_ref,
                       mk_ref, mv_ref, wdw_ref, bdw_ref, lng_ref, lnb_ref,
                       a_ref, b_ref, m_ref, ext_ref):
    tm = glu_ref.shape[1]
    l0, l1, l2 = l0_ref[0], l1_ref[0], l2_ref[0]
    lm = jnp.maximum(jnp.maximum(l0, l1), l2)
    w0, w1, w2 = jnp.exp(l0 - lm), jnp.exp(l1 - lm), jnp.exp(l2 - lm)
    a = (w0 * o0_ref[0] + w1 * o1_ref[0] + w2 * o2_ref[0]) / (w0 + w1 + w2)
    a_ref[0] = a.astype(BF16)
    keep = jnp.where(pl.program_id(1) == 0, 0.0, 1.0)
    ext_ref[0:CONV_HALO, :] = halo_ref[0] * keep
    ext_ref[CONV_HALO:, :] = glu_ref[0]
    base = CONV_HALO - (CONV_W - 1)
    y = None
    for b in range(SUBLANES):
        zb = None
        for a in range(CONV_HALO // SUBLANES):
            k = SUBLANES * a + b - base
            if 0 <= k < CONV_W:
                term = ext_ref[pl.ds(SUBLANES * a, tm + SUBLANES), :] * wdw_ref[k:k + 1, :]
                zb = term if zb is None else zb + term
        y = zb[b:b + tm] if y is None else y + zb[b:b + tm]
    y = y + ext_ref[pl.ds(CONV_HALO, tm), :] * wdw_ref[CONV_W - 1:CONV_W, :]
    b_ref[0] = _conv_tail(y, bdw_ref[...], lng_ref[...], lnb_ref[...]).astype(BF16)
    for h in range(N_HEADS_M):
        sl = slice(h * HEAD_DIM, (h + 1) * HEAD_DIM)
        m_ref[0, :, sl] = _mem_attend(qm_ref[0, :, sl], mk_ref[0, :, sl].astype(BF16),
                                      mv_ref[0, :, sl].astype(BF16)).astype(BF16)


def _prompt_mix(os_, ls_, glu, qm, mk, mv, wdw, bdw, lng, lnb, tm):
    B, S, W = glu.shape
    M = mk.shape[1]
    row = lambda b, i: (b, i, 0)
    halo = lambda b, i: (b, jnp.maximum(i * (tm // CONV_HALO) - 1, 0), 0)
    per_b = lambda b, i: (b, 0, 0)
    const = lambda b, i: (0, 0)
    blk = pl.BlockSpec((1, tm, W), row)
    return pl.pallas_call(
        _prompt_mix_kernel,
        grid=(B, S // tm),
        in_specs=[blk] * 7 + [pl.BlockSpec((1, CONV_HALO, W), halo), blk,
                              pl.BlockSpec((1, M, W), per_b), pl.BlockSpec((1, M, W), per_b),
                              pl.BlockSpec((CONV_W, W), const), pl.BlockSpec((1, W), const),
                              pl.BlockSpec((1, W), const), pl.BlockSpec((1, W), const)],
        out_specs=[blk] * 3,
        out_shape=[jax.ShapeDtypeStruct((B, S, W), BF16)] * 3,
        scratch_shapes=[pltpu.VMEM((tm + CONV_HALO, W), F32)],
        compiler_params=_cparams(("parallel", "parallel")),
    )(*os_, *ls_, glu, glu, qm, mk, mv, wdw, bdw, lng, lnb)


SROWS = 8


def _sample_mix_kernel(q0_ref, q1_ref, q2_ref, kn_ref, vn_ref, glu_ref, qm_ref, wk_ref, wv_ref,
                       mk_ref, mv_ref, st_ref, wdw_ref, bdw_ref, lng_ref, lnb_ref,
                       a_ref, b_ref, m_ref, nst_ref):
    n_buf = wk_ref.shape[1] // HEADS_PER_GROUP
    n_mem = mk_ref.shape[1] // N_HEADS_M
    head = lambda ref, j, n: ref[0, pl.ds(j, n, stride=HEADS_PER_GROUP), :].astype(BF16)
    row = lax.broadcasted_iota(jnp.int32, (SROWS, n_buf), 0)
    dist = n_buf - lax.broadcasted_iota(jnp.int32, (SROWS, n_buf), 1)
    (w0, d0), (w1, d1), (w2, d2) = DIL_GROUPS
    dmask = jnp.where(row == 0, d0 - 1, jnp.where(row == 1, d1 - 1, d2 - 1))
    win = jnp.where(row == 0, w0, jnp.where(row == 1, w1, w2))
    valid = jnp.logical_and((dist & dmask) == 0, dist <= win)
    for j in range(HEADS_PER_GROUP):
        sl = slice(j * HEAD_DIM, (j + 1) * HEAD_DIM)
        q8 = jnp.where(row[:, :HEAD_DIM] == 0, q0_ref[0, :, sl],
                       jnp.where(row[:, :HEAD_DIM] == 1, q1_ref[0, :, sl], q2_ref[0, :, sl]))
        s = jnp.where(valid, _dot_nt(q8.astype(BF16), head(wk_ref, j, n_buf)) * SCALE, NEG)
        s_new = jnp.sum(q8 * kn_ref[0, :, sl], axis=-1, keepdims=True) * SCALE
        mx = jnp.maximum(jnp.max(s, axis=-1, keepdims=True), s_new)
        p = jnp.exp(s - mx)
        p_new = jnp.exp(s_new - mx)
        den = jnp.sum(p, axis=-1, keepdims=True) + p_new
        o = (_dot(p.astype(BF16), head(wv_ref, j, n_buf)) + p_new * vn_ref[0, :, sl]) / den
        lse = mx + jnp.log(den)
        lm = jnp.maximum(jnp.maximum(lse[0:1], lse[1:2]), lse[2:3])
        g0, g1, g2 = jnp.exp(lse[0:1] - lm), jnp.exp(lse[1:2] - lm), jnp.exp(lse[2:3] - lm)
        a_ref[0, :, sl] = (g0 * o[0:1] + g1 * o[1:2] + g2 * o[2:3]) / (g0 + g1 + g2)
        qm8 = jnp.broadcast_to(qm_ref[0, :, sl], (SROWS, HEAD_DIM)).astype(BF16)
        mo = _mem_attend(qm8, head(mk_ref, j, n_mem), head(mv_ref, j, n_mem))
        m_ref[0, :, sl] = mo[0:1]
    glu = glu_ref[0]
    y = jnp.sum(st_ref[0] * wdw_ref[0:CONV_W - 1, :], axis=0, keepdims=True) + glu * wdw_ref[CONV_W - 1:CONV_W, :]
    b_ref[0] = _conv_tail(y, bdw_ref[...], lng_ref[...], lnb_ref[...])
    nst_ref[0, 0:CONV_W - 2, :] = st_ref[0, 1:CONV_W - 1, :]
    nst_ref[0, CONV_W - 2:CONV_W - 1, :] = glu


def _sample_mix(qs, kn, vn, glu, qm, wk, wv, mk, mv, st, wdw, bdw, lng, lnb):
    N, W = st.shape[0], st.shape[2]
    one = pl.BlockSpec((1, 1, W), lambda b: (b, 0, 0))
    per = lambda r: pl.BlockSpec((1, r, W), lambda b: (b, 0, 0))
    cache = lambda t: pl.BlockSpec((1,) + t.shape[1:], lambda b: (b, 0, 0))
    const = lambda b: (0, 0)
    return pl.pallas_call(
        _sample_mix_kernel,
        grid=(N,),
        in_specs=[one] * 7 + [cache(wk), cache(wv), cache(mk), cache(mv), per(CONV_W - 1),
                              pl.BlockSpec((CONV_W, W), const), pl.BlockSpec((1, W), const),
                              pl.BlockSpec((1, W), const), pl.BlockSpec((1, W), const)],
        out_specs=[one, one, one, per(CONV_W - 1)],
        out_shape=[jax.ShapeDtypeStruct((N, 1, W), F32)] * 3 + [jax.ShapeDtypeStruct((N, CONV_W - 1, W), F32)],
        compiler_params=_cparams(("parallel",)),
    )(*qs, kn, vn, glu, qm, wk, wv, mk, mv, st, wdw, bdw, lng, lnb)


def _back_kernel(x_ref, a_ref, b_ref, m_ref, g1_ref, wg_ref, bg_ref, wa_ref, wb_ref, wm_ref, wo_ref, g2_ref,
                 x1_ref, h2_ref):
    x = x_ref[...]
    hb = _rms(x, g1_ref[...]).astype(BF16)
    merged = None
    for i, (t_ref, w_ref) in enumerate(((a_ref, wa_ref), (b_ref, wb_ref), (m_ref, wm_ref))):
        sl = slice(i * D_MODEL, (i + 1) * D_MODEL)
        gate = _sigmoid(_dot(hb, wg_ref[:, sl]) + bg_ref[:, sl])
        term = gate * _dot(t_ref[...], w_ref[...])
        merged = term if merged is None else merged + term
    x1 = x + _dot(merged.astype(BF16), wo_ref[...])
    x1_ref[...] = x1
    h2_ref[...] = _pack_rows(_rms(x1, g2_ref[...]).astype(BF16))


def _back(x, a, b, m, g1, wg, bg, wa, wb, wm, wo, g2, tm):
    N, D = x.shape
    W = a.shape[1]
    row = lambda i: (i, 0)
    const = lambda i: (0, 0)
    return pl.pallas_call(
        _back_kernel,
        grid=(N // tm,),
        in_specs=[pl.BlockSpec((tm, D), row)] + [pl.BlockSpec((tm, W), row)] * 3 +
                 [pl.BlockSpec((1, D), const), pl.BlockSpec((D, 3 * D), const), pl.BlockSpec((1, 3 * D), const),
                  pl.BlockSpec((W, D), const), pl.BlockSpec((W, D), const), pl.BlockSpec((W, D), const),
                  pl.BlockSpec((D, D), const), pl.BlockSpec((1, D), const)],
        out_specs=[pl.BlockSpec((tm, D), row), pl.BlockSpec((tm // 2, D), row)],
        out_shape=[jax.ShapeDtypeStruct((N, D), F32), jax.ShapeDtypeStruct((N // 2, D), jnp.int32)],
        compiler_params=_cparams(("parallel",)),
    )(x, a, b, m, g1, wg, bg, wa, wb, wm, wo, g2)


PEER_CAND_ROWS = 56


def _top_sorted(s, k):
    out = []
    for _ in range(k):
        m = jnp.max(s, axis=0, keepdims=True)
        out.append(m)
        s = jnp.where(s == m, NEG, s)
    return out


def _prefix_count(test, t):
    c8 = test(t[7])
    c4 = test(jnp.where(c8, t[11], t[3]))
    c2 = test(jnp.where(c8, jnp.where(c4, t[13], t[9]), jnp.where(c4, t[5], t[1])))
    c1 = test(jnp.where(c8, jnp.where(c4, jnp.where(c2, t[14], t[12]), jnp.where(c2, t[10], t[8])),
                        jnp.where(c4, jnp.where(c2, t[6], t[4]), jnp.where(c2, t[2], t[0]))))
    n = (jnp.where(c8, 8.0, 0.0) + jnp.where(c4, 4.0, 0.0)) + (jnp.where(c2, 2.0, 0.0) + jnp.where(c1, 1.0, 0.0))
    return jnp.where(test(t[15]), 16.0, n)


def _peer_route_kernel(h_ref, wq_ref, sk_ref, r2_ref, e2_ref, qd_ref, e1d_ref, cand_ref):
    hb = _unpack_rows(h_ref[...])
    pairs = [(p, q) for p in range(PEER_TOPK) for q in range(PEER_TOPK) if (p + 1) * (q + 1) <= PEER_TOPK]
    cand_ref[...] = jnp.full(cand_ref.shape, NEG, F32)
    for h in range(PEER_HEADS):
        r0 = h * 2 * N_KEYS
        q1 = _dot_nt(wq_ref[r0:r0 + N_KEYS, :], hb).astype(BF16)
        q2 = _dot_nt(wq_ref[r0 + N_KEYS:r0 + 2 * N_KEYS, :], hb).astype(BF16)
        s1 = _dot(sk_ref[0], q1)
        s2 = _dot(sk_ref[1], q2)
        t1 = _top_sorted(s1, PEER_TOPK)
        t2 = _top_sorted(s2, PEER_TOPK)
        for r, (p, q) in enumerate(pairs):
            cand_ref[r:r + 1, :] = t1[p] + t2[q]
        top = _top_sorted(cand_ref[...], PEER_TOPK)
        z = jnp.exp(top[0] - top[0])
        for k in range(1, PEER_TOPK):
            z = z + jnp.exp(top[k] - top[0])
        tau = top[PEER_TOPK - 1]
        r2_ref[h] = _pack_rows(_prefix_count(lambda v: v > s2, t2).astype(BF16))
        qd_ref[h] = _prefix_count(lambda v: s1 + v >= tau, t2)
        e1d_ref[h] = jnp.exp(s1 - t1[0])
        e2_ref[h] = _pack_rows((jnp.exp(s2 - t2[0]) * (0.5 / z)).astype(BF16))


def _peer_route(h2, wq_t, sk, t):
    N, D = 2 * h2.shape[0], h2.shape[1]
    blk = lambda r: pl.BlockSpec((PEER_HEADS, r, t), lambda i: (0, 0, i))
    sds = lambda r: jax.ShapeDtypeStruct((PEER_HEADS, r, N), jnp.int32 if r < N_KEYS else F32)
    return pl.pallas_call(
        _peer_route_kernel,
        grid=(N // t,),
        in_specs=[pl.BlockSpec((t // 2, D), lambda i: (i, 0)),
                  pl.BlockSpec(wq_t.shape, lambda i: (0, 0)),
                  pl.BlockSpec(sk.shape, lambda i: (0, 0, 0))],
        out_specs=[blk(N_KEYS // 2), blk(N_KEYS // 2), blk(N_KEYS), blk(N_KEYS)],
        out_shape=[sds(N_KEYS // 2), sds(N_KEYS // 2), sds(N_KEYS), sds(N_KEYS)],
        scratch_shapes=[pltpu.VMEM((PEER_CAND_ROWS, t), F32)],
        compiler_params=_cparams(("parallel",)),
    )(h2, wq_t, sk)


PEER_CI = 16
PEER_SUB = 4
PEER_LANES = 256
PEER_W_LANES = 128
PEER_OUT_SPLIT = 2
GELU_C = 0.7978845608028654
GELU_A = 0.044715


def _peer_dense_kernel(h_ref, x1_ref, u_ref, vt_ref, r2_ref, e2_ref, qd_ref, e1d_ref, y_ref, acc_ref):
    c = pl.program_id(1)
    t = x1_ref.shape[0]
    lanes = min(PEER_LANES, t)
    sub = PEER_SUB * N_KEYS
    n_sub = PEER_CI // PEER_SUB

    @pl.when(c == 0)
    def _():
        acc_ref[...] = jnp.zeros_like(acc_ref)

    wl = min(PEER_W_LANES, lanes)

    def row_bf16(row, w0):
        return jnp.broadcast_to(row[:, w0:w0 + wl], (N_KEYS, wl)).astype(BF16)

    zero = jnp.zeros((N_KEYS, wl), BF16)
    subs = [(l0, q) for l0 in range(0, t, lanes) for q in range(n_sub)]

    def first_matmul(k):
        l0, q = subs[k]
        u = _unpack_rows(u_ref[q * sub // 2:(q + 1) * sub // 2, :])
        return _dot_nt(u, _unpack_rows(h_ref[l0 // 2:(l0 + lanes) // 2, :]))

    def gated(k, at):
        l0, q = subs[k]
        rows = []
        for ii in range(PEER_SUB):
            i = c * PEER_CI + q * PEER_SUB + ii
            counts = [qd_ref[h, pl.ds(i, 1), l0:l0 + lanes] for h in range(PEER_HEADS)]
            e1s = [e1d_ref[h, pl.ds(i, 1), l0:l0 + lanes] for h in range(PEER_HEADS)]
            cols = []
            for w0 in range(0, lanes, wl):
                ls = slice(l0 + w0, l0 + w0 + wl)
                w = None
                for h in range(PEER_HEADS):
                    sel = _unpack_rows(r2_ref[h, :, ls]) < row_bf16(counts[h], w0)
                    term = jnp.where(sel, _unpack_rows(e2_ref[h, :, ls]), zero) * row_bf16(e1s[h], w0)
                    w = term if w is None else w + term
                a = at[ii * N_KEYS:(ii + 1) * N_KEYS, w0:w0 + wl]
                inner = a * ((a * a) * (GELU_C * GELU_A) + GELU_C)
                cols.append(w * (a * jnp.tanh(inner) + a).astype(BF16))
            rows.append(cols[0] if len(cols) == 1 else jnp.concatenate(cols, axis=1))
        return jnp.concatenate(rows, axis=0)

    parts = {}

    d_model = acc_ref.shape[0]
    out_rows = d_model // PEER_OUT_SPLIT

    def second_matmul(k, hk):
        l0, q = subs[k]
        for o0 in range(0, d_model, out_rows):
            d = _dot(_unpack_rows(vt_ref[o0 // 2:(o0 + out_rows) // 2, q * sub:(q + 1) * sub]), hk)
            parts[l0, o0] = d if q == 0 else parts[l0, o0] + d
            if q == n_sub - 1:
                acc_ref[o0:o0 + out_rows, l0:l0 + lanes] += parts[l0, o0]

    at_next = first_matmul(0)
    h_prev = None
    for k in range(len(subs)):
        at = at_next
        if k + 1 < len(subs):
            at_next = first_matmul(k + 1)
        if k > 0:
            second_matmul(k - 1, h_prev)
        h_prev = gated(k, at)
    second_matmul(len(subs) - 1, h_prev)

    @pl.when(c == pl.num_programs(1) - 1)
    def _():
        y_ref[...] = x1_ref[...] + acc_ref[...].T


def _peer_dense(h2, x1, u_p, vt_p, r2, e2, qd, e1d, t):
    N, D = x1.shape
    E = 2 * u_p.shape[0]
    ce = PEER_CI * N_KEYS
    tok = lambda i, c: (i, 0)
    rt = lambda r: pl.BlockSpec((PEER_HEADS, r, t), lambda i, c: (0, 0, i))
    return pl.pallas_call(
        _peer_dense_kernel,
        grid=(N // t, E // ce),
        in_specs=[pl.BlockSpec((t // 2, D), tok), pl.BlockSpec((t, D), tok),
                  pl.BlockSpec((ce // 2, D), lambda i, c: (c, 0)),
                  pl.BlockSpec((D // 2, ce), lambda i, c: (0, c)),
                  rt(N_KEYS // 2), rt(N_KEYS // 2), rt(N_KEYS), rt(N_KEYS)],
        out_specs=pl.BlockSpec((t, D), tok),
        out_shape=jax.ShapeDtypeStruct((N, D), F32),
        scratch_shapes=[pltpu.VMEM((D, t), F32)],
        compiler_params=_cparams(("parallel", "arbitrary")),
    )(h2, x1, u_p, vt_p, r2, e2, qd, e1d)


def _peer(h2, x1, wq_t, sk, u_p, vt_p, t):
    N = x1.shape[0]
    pad = -N % t
    if pad:
        h2 = jnp.pad(h2, ((0, pad // 2), (0, 0)))
        x1 = jnp.pad(x1, ((0, pad), (0, 0)))
    routed = _peer_route(h2, wq_t, sk, min(t, PEER_LANES))
    return _peer_dense(h2, x1, u_p, vt_p, *routed, t)[:N]


def kernel(x_prompt, x_sample, mem_prompt, cache_win_k, cache_win_v, cache_mem_k, cache_mem_v, state_conv, g_norm1, w_in, b_gate, qn_a, kn_a, qn_m, kn_m, g_mem, w_mem_kv, w_dw, b_dw, ln_g, ln_b, w_a_proj, w_b_proj, w_m_proj, w_o, g_norm2, w_pq, sub_keys, u_tab, v_tab):
    B, S, D = x_prompt.shape
    NS = x_sample.shape[0]
    assert cache_win_k.shape[0] == 1 and x_sample.shape[1] == 1
    n_win = min(DIL_GROUPS[-1][0], S)
    tm = min(512, S)
    tp = min(512, B * S)

    w_main = w_in[0, :, :IN_MAIN_W].astype(BF16)
    w_gate = w_in[0, :, IN_MAIN_W:].astype(BF16)
    row = lambda v: v.reshape(1, -1)
    g1, g2, bg = row(g_norm1[0]), row(g_norm2[0]), row(b_gate[0])
    qna, kna, qnm, knm = row(qn_a[0]), row(kn_a[0]), row(qn_m[0]), row(kn_m[0])
    bdw, lng, lnb = row(b_dw[0]), row(ln_g[0]), row(ln_b[0])
    wdw = w_dw[0]
    wa, wb, wm, wo = (w[0].astype(BF16) for w in (w_a_proj, w_b_proj, w_m_proj, w_o))
    wq_t = w_pq[0].T.astype(BF16)
    sk = sub_keys[0].astype(BF16)
    u_b = _pack_table(u_tab[0], transpose=False)
    vt_b = _pack_table(v_tab[0], transpose=True)

    q0, q1, q2, kaf, vaf, glu, qm = _front(x_prompt, jnp.arange(S), g1, w_main, qna, kna, qnm, tm)
    outs = [_band_group(q, kaf, vaf, dil, max(BAND_STEP_ROWS, dil * BAND_BLK))
            for q, (_, dil) in zip((q0, q1, q2), DIL_GROUPS)]
    mk, mv = _mem_kv(mem_prompt, row(g_mem[0]), w_mem_kv[0].astype(BF16), knm)
    a, b, m = _prompt_mix([o for o, _ in outs], [l for _, l in outs], glu, qm, mk, mv, wdw, bdw, lng, lnb, tm)
    flat = lambda t: t.reshape(B * S, -1)
    x1, h2 = _back(flat(x_prompt), flat(a), flat(b), flat(m), g1, w_gate, bg, wa, wb, wm, wo, g2, tm)
    y_prompt = _peer(h2, x1, wq_t, sk, u_b, vt_b, tp).reshape(B, S, D)

    xs = x_sample.reshape(1, NS, D)
    pos_s = jnp.full((NS,), PAST_LEN, jnp.int32)
    sq0, sq1, sq2, skaf, svaf, sglu, sqm = _front(xs, pos_s, g1, w_main, qna, kna, qnm, NS)
    tok = lambda t: t.reshape(NS, 1, -1).astype(F32)
    cw = lambda t: t.reshape(NS, -1, HEAD_DIM)
    sa, sb, sm, new_conv_s = _sample_mix(
        (tok(sq0), tok(sq1), tok(sq2)), tok(skaf), tok(svaf), tok(sglu), tok(sqm),
        cw(cache_win_k), cw(cache_win_v), cw(cache_mem_k), cw(cache_mem_v), state_conv[0],
        wdw, bdw, lng, lnb)
    mixed = lambda t: t.reshape(NS, -1).astype(BF16)
    sx1, sh2 = _back(x_sample.reshape(NS, D), mixed(sa), mixed(sb), mixed(sm),
                     g1, w_gate, bg, wa, wb, wm, wo, g2, NS)
    y_sample = _peer(sh2, sx1, wq_t, sk, u_b, vt_b, 128).reshape(NS, 1, D)

    heads = lambda t, n: t.reshape(1, t.shape[0], n, HEADS_PER_GROUP, HEAD_DIM)
    return (y_prompt, y_sample,
            heads(kaf[:, S - n_win:], n_win), heads(vaf[:, S - n_win:], n_win),
            heads(mk, N_MEM), heads(mv, N_MEM),
            glu[None, :, S - (CONV_W - 1):],
            heads(skaf.reshape(NS, 1, -1), 1), heads(svaf.reshape(NS, 1, -1), 1),
            new_conv_s[None])
```

```python
import functools

import jax
import jax.numpy as jnp
from jax import lax
from jax.experimental import pallas as pl
from jax.experimental.pallas import tpu as pltpu

D_MODEL = 1024
HEAD_DIM = 128
ROT_DIM = HEAD_DIM // 4
ROPE_THETA = 500000.0
DIL_GROUPS = ((128, 1), (512, 4), (2048, 16))
HEADS_PER_GROUP = 4
N_GROUPS = len(DIL_GROUPS)
BAND_BLK = 128
CONV_CH = 512
CONV_W = 31
N_HEADS_M = 4
N_MEM = 256
PEER_HEADS = 8
N_KEYS = 128
PEER_TOPK = 16
PAST_LEN = 16384
EPS = 1e-6

GROUP_W = HEADS_PER_GROUP * HEAD_DIM
QA_W = N_GROUPS * GROUP_W
IN_MAIN_W = QA_W + 2 * GROUP_W + 2 * CONV_CH + N_HEADS_M * HEAD_DIM
OFF_KA = QA_W
OFF_VA = OFF_KA + GROUP_W
OFF_U = OFF_VA + GROUP_W
OFF_QM = OFF_U + 2 * CONV_CH
SCALE = HEAD_DIM ** -0.5
NEG = -1e30
SUBLANES = 8
CONV_HALO = 32
BAND_UNIT_BATCH = 4
BAND_STEP_ROWS = 2048

VMEM_LIMIT = 56 * 1024 * 1024
BF16 = jnp.bfloat16
F32 = jnp.float32

NT_DIMS = (((1,), (1,)), ((), ()))


def _cparams(sem):
    return pltpu.CompilerParams(dimension_semantics=sem, vmem_limit_bytes=VMEM_LIMIT)


def _rms(x, g):
    return x * lax.rsqrt(jnp.mean(x * x, axis=-1, keepdims=True) + EPS) * g


def _sigmoid(x):
    return 1.0 / (1.0 + jnp.exp(-x))


def _dot(a, b):
    return jnp.dot(a, b, preferred_element_type=F32)


def _dot_nt(a, b):
    return lax.dot_general(a, b, NT_DIMS, preferred_element_type=F32)


def _pack_rows(x):
    return pltpu.bitcast(x, jnp.int32)


def _unpack_rows(x):
    return pltpu.bitcast(x, BF16)


def _pack_table_kernel(x_ref, o_ref, *, transpose):
    x = x_ref[...]
    o_ref[...] = _pack_rows((x.T if transpose else x).astype(BF16))


def _pack_table(x, transpose, rows=512):
    R, C = x.shape
    if transpose:
        out_shape, out_spec = (C // 2, R), pl.BlockSpec((C // 2, rows), lambda i: (0, i))
    else:
        out_shape, out_spec = (R // 2, C), pl.BlockSpec((rows // 2, C), lambda i: (i, 0))
    return pl.pallas_call(
        functools.partial(_pack_table_kernel, transpose=transpose),
        grid=(R // rows,),
        in_specs=[pl.BlockSpec((rows, C), lambda i: (i, 0))],
        out_specs=out_spec,
        out_shape=jax.ShapeDtypeStruct(out_shape, jnp.int32),
        compiler_params=_cparams(("parallel",)),
    )(x)


def _front_kernel(x_ref, g1_ref, w_ref, qna_ref, kna_ref, qnm_ref, rc_ref, rsa_ref, rsb_ref,
                  qa0_ref, qa1_ref, qa2_ref, kaf_ref, vaf_ref, glu_ref, qm_ref):
    x = x_ref[0]
    hb = _rms(x, g1_ref[...]).astype(BF16)
    rc, rsa, rsb = rc_ref[...], rsa_ref[...], rsb_ref[...]

    def rot(y):
        return y * rc + pltpu.roll(y, HEAD_DIM - ROT_DIM // 2, 1) * rsa + pltpu.roll(y, ROT_DIM // 2, 1) * rsb

    qna, kna, qnm = qna_ref[...], kna_ref[...], qnm_ref[...]
    for g, q_ref in enumerate((qa0_ref, qa1_ref, qa2_ref)):
        p = _dot(hb, w_ref[:, g * GROUP_W:(g + 1) * GROUP_W])
        for h in range(HEADS_PER_GROUP):
            sl = slice(h * HEAD_DIM, (h + 1) * HEAD_DIM)
            q_ref[0, :, sl] = rot(_rms(p[:, sl], qna))
    p = _dot(hb, w_ref[:, OFF_KA:OFF_KA + GROUP_W])
    for h in range(HEADS_PER_GROUP):
        sl = slice(h * HEAD_DIM, (h + 1) * HEAD_DIM)
        kaf_ref[0, :, sl] = rot(_rms(p[:, sl], kna))
    vaf_ref[0] = _dot(hb, w_ref[:, OFF_VA:OFF_VA + GROUP_W])
    p = _dot(hb, w_ref[:, OFF_U:OFF_U + 2 * CONV_CH])
    glu_ref[0] = p[:, :CONV_CH] * _sigmoid(p[:, CONV_CH:])
    p = _dot(hb, w_ref[:, OFF_QM:OFF_QM + N_HEADS_M * HEAD_DIM])
    for h in range(N_HEADS_M):
        sl = slice(h * HEAD_DIM, (h + 1) * HEAD_DIM)
        qm_ref[0, :, sl] = _rms(p[:, sl], qnm).astype(BF16)


def _rope_tables(pos):
    half = ROT_DIM // 2
    inv = jnp.float32(ROPE_THETA) ** (-jnp.arange(half, dtype=F32) / half)
    ang = pos.astype(F32)[:, None] * inv[None, :]
    cos, sin = jnp.cos(ang), jnp.sin(ang)
    n = pos.shape[0]
    rest = HEAD_DIM - ROT_DIM
    rc = jnp.concatenate([cos, cos, jnp.ones((n, rest), F32)], axis=1)
    rsa = jnp.concatenate([-sin, jnp.zeros((n, half + rest), F32)], axis=1)
    rsb = jnp.concatenate([jnp.zeros((n, half), F32), sin, jnp.zeros((n, rest), F32)], axis=1)
    return rc, rsa, rsb


def _front(x, pos, g1, w_main, qna, kna, qnm, tm):
    B, S, D = x.shape
    rc, rsa, rsb = _rope_tables(pos)
    row = lambda b, i: (b, i, 0)
    const = lambda b, i: (0, 0)
    tab = lambda b, i: (i, 0)
    blk = pl.BlockSpec((1, tm, GROUP_W), row)
    sds = lambda dt: jax.ShapeDtypeStruct((B, S, GROUP_W), dt)
    return pl.pallas_call(
        _front_kernel,
        grid=(B, S // tm),
        in_specs=[pl.BlockSpec((1, tm, D), row),
                  pl.BlockSpec((1, D), const),
                  pl.BlockSpec((D, IN_MAIN_W), const),
                  pl.BlockSpec((1, HEAD_DIM), const),
                  pl.BlockSpec((1, HEAD_DIM), const),
                  pl.BlockSpec((1, HEAD_DIM), const),
                  pl.BlockSpec((tm, HEAD_DIM), tab),
                  pl.BlockSpec((tm, HEAD_DIM), tab),
                  pl.BlockSpec((tm, HEAD_DIM), tab)],
        out_specs=[blk] * 7,
        out_shape=[sds(F32)] * 6 + [sds(BF16)],
        compiler_params=_cparams(("parallel", "parallel")),
    )(x, g1, w_main, qna, kna, qnm, rc, rsa, rsb)


def _band_kernel(q_ref, kc_ref, kp_ref, vc_ref, vp_ref, o_ref, lse_ref, *, dil):
    span = dil * BAND_BLK
    n_units = q_ref.shape[1] // span
    qi = lax.broadcasted_iota(jnp.int32, (BAND_BLK, BAND_BLK), 0)
    ki = lax.broadcasted_iota(jnp.int32, (BAND_BLK, BAND_BLK), 1)
    mask_cur = ki <= qi
    mask_prev = ki >= qi
    mask_first = ki >= qi + jnp.where(pl.program_id(1) > 0, 0, BAND_BLK)

    def rows(ref, r, u):
        return ref[0, pl.ds(r + u * span, BAND_BLK, stride=dil), :] if dil > 1 else ref[0, pl.ds(u * span, BAND_BLK), :]

    def put(ref, r, u, val):
        if dil > 1:
            ref[0, pl.ds(r + u * span, BAND_BLK, stride=dil), :] = val
        else:
            ref[0, pl.ds(u * span, BAND_BLK), :] = val

    units = [(r, u) for r in range(dil) for u in range(n_units)]
    for b0 in range(0, len(units), BAND_UNIT_BATCH):
        batch = units[b0:b0 + BAND_UNIT_BATCH]
        scores, values = [], []
        for r, u in batch:
            q = rows(q_ref, r, u).astype(BF16)
            kc = rows(kc_ref, r, u).astype(BF16)
            if u > 0:
                kp, vp, mp = rows(kc_ref, r, u - 1), rows(vc_ref, r, u - 1), mask_prev
            else:
                kp, vp, mp = rows(kp_ref, r, 0), rows(vp_ref, r, 0), mask_first
            scores.append((jnp.where(mask_cur, _dot_nt(q, kc) * SCALE, NEG),
                           jnp.where(mp, _dot_nt(q, kp.astype(BF16)) * SCALE, NEG)))
            values.append((rows(vc_ref, r, u).astype(BF16), vp.astype(BF16)))
        stats = []
        for sc, sp in scores:
            m = jnp.maximum(jnp.max(sc, axis=-1, keepdims=True), jnp.max(sp, axis=-1, keepdims=True))
            pc = jnp.exp(sc - m)
            pp = jnp.exp(sp - m)
            den = jnp.sum(pc, axis=-1, keepdims=True) + jnp.sum(pp, axis=-1, keepdims=True)
            stats.append((pc.astype(BF16), pp.astype(BF16), m, den))
        for (r, u), (pc, pp, m, den), (vc, vp) in zip(batch, stats, values):
            put(o_ref, r, u, (_dot(pc, vc) + _dot(pp, vp)) / den)
            put(lse_ref, r, u, jnp.broadcast_to(m + jnp.log(den), (BAND_BLK, HEAD_DIM)))


def _band_group(q, k, v, dil, rows_per_step):
    B, S, W = q.shape
    span = dil * BAND_BLK
    rb = min(rows_per_step, S)
    assert rb % span == 0 and S % rb == 0
    cur = lambda b, i, h: (b, i, h)
    prev = lambda b, i, h: (b, jnp.maximum(i * (rb // span) - 1, 0), h)
    blk = pl.BlockSpec((1, rb, HEAD_DIM), cur)
    pblk = pl.BlockSpec((1, span, HEAD_DIM), prev)
    return pl.pallas_call(
        functools.partial(_band_kernel, dil=dil),
        grid=(B, S // rb, W // HEAD_DIM),
        in_specs=[blk, blk, pblk, blk, pblk],
        out_specs=[blk, blk],
        out_shape=[jax.ShapeDtypeStruct((B, S, W), F32)] * 2,
        compiler_params=_cparams(("parallel", "parallel", "parallel")),
    )(q, k, k, v, v)


def _mem_kv_kernel(mem_ref, g_ref, w_ref, kn_ref, mk_ref, mv_ref):
    hb = _rms(mem_ref[0], g_ref[...]).astype(BF16)
    kv = _dot(hb, w_ref[...])
    kn = kn_ref[...]
    width = N_HEADS_M * HEAD_DIM
    for h in range(N_HEADS_M):
        sl = slice(h * HEAD_DIM, (h + 1) * HEAD_DIM)
        mk_ref[0, :, sl] = _rms(kv[:, sl], kn)
    mv_ref[0] = kv[:, width:]


def _mem_kv(mem, g_mem, w_mem_kv, kn_m):
    B, M, D = mem.shape
    width = N_HEADS_M * HEAD_DIM
    const = lambda b: (0, 0)
    return pl.pallas_call(
        _mem_kv_kernel,
        grid=(B,),
        in_specs=[pl.BlockSpec((1, M, D), lambda b: (b, 0, 0)),
                  pl.BlockSpec((1, D), const),
                  pl.BlockSpec((D, 2 * width), const),
                  pl.BlockSpec((1, HEAD_DIM), const)],
        out_specs=[pl.BlockSpec((1, M, width), lambda b: (b, 0, 0))] * 2,
        out_shape=[jax.ShapeDtypeStruct((B, M, width), F32)] * 2,
        compiler_params=_cparams(("parallel",)),
    )(mem, g_mem, w_mem_kv, kn_m)


def _conv_tail(y, bdw, lng, lnb):
    yf = y + bdw
    mu = jnp.mean(yf, axis=-1, keepdims=True)
    c = yf - mu
    var = jnp.mean(c * c, axis=-1, keepdims=True)
    yn = c * lax.rsqrt(var + EPS) * lng + lnb
    return yn * _sigmoid(yn)


def _mem_attend(q, mk, mv):
    s = _dot_nt(q, mk) * SCALE
    p = jnp.exp(s - jnp.max(s, axis=-1, keepdims=True))
    return _dot(p.astype(BF16), mv) / jnp.sum(p, axis=-1, keepdims=True)


def _prompt_mix_kernel(o0_ref, o1_ref, o2_ref, l0_ref, l1_ref, l2_ref, glu_ref, halo_ref, qm_ref,
                       mk_ref, mv_ref, wdw_ref, bdw_ref, lng_ref, lnb_ref,
                       a_ref, b_ref, m_ref, ext_ref):
    tm = glu_ref.shape[1]
    l0, l1, l2 = l0_ref[0], l1_ref[0], l2_ref[0]
    lm = jnp.maximum(jnp.maximum(l0, l1), l2)
    w0, w1, w2 = jnp.exp(l0 - lm), jnp.exp(l1 - lm), jnp.exp(l2 - lm)
    a = (w0 * o0_ref[0] + w1 * o1_ref[0] + w2 * o2_ref[0]) / (w0 + w1 + w2)
    a_ref[0] = a.astype(BF16)
    keep = jnp.where(pl.program_id(1) == 0, 0.0, 1.0)
    ext_ref[0:CONV_HALO, :] = halo_ref[0] * keep
    ext_ref[CONV_HALO:, :] = glu_ref[0]
    base = CONV_HALO - (CONV_W - 1)
    y = None
    for b in range(SUBLANES):
        zb = None
        for a in range(CONV_HALO // SUBLANES):
            k = SUBLANES * a + b - base
            if 0 <= k < CONV_W:
                term = ext_ref[pl.ds(SUBLANES * a, tm + SUBLANES), :] * wdw_ref[k:k + 1, :]
                zb = term if zb is None else zb + term
        y = zb[b:b + tm] if y is None else y + zb[b:b + tm]
    y = y + ext_ref[pl.ds(CONV_HALO, tm), :] * wdw_ref[CONV_W - 1:CONV_W, :]
    b_ref[0] = _conv_tail(y, bdw_ref[...], lng_ref[...], lnb_ref[...]).astype(BF16)
    for h in range(N_HEADS_M):
        sl = slice(h * HEAD_DIM, (h + 1) * HEAD_DIM)
        m_ref[0, :, sl] = _mem_attend(qm_ref[0, :, sl], mk_ref[0, :, sl].astype(BF16),
                                      mv_ref[0, :, sl].astype(BF16)).astype(BF16)


def _prompt_mix(os_, ls_, glu, qm, mk, mv, wdw, bdw, lng, lnb, tm):
    B, S, W = glu.shape
    M = mk.shape[1]
    row = lambda b, i: (b, i, 0)
    halo = lambda b, i: (b, jnp.maximum(i * (tm // CONV_HALO) - 1, 0), 0)
    per_b = lambda b, i: (b, 0, 0)
    const = lambda b, i: (0, 0)
    blk = pl.BlockSpec((1, tm, W), row)
    return pl.pallas_call(
        _prompt_mix_kernel,
        grid=(B, S // tm),
        in_specs=[blk] * 7 + [pl.BlockSpec((1, CONV_HALO, W), halo), blk,
                              pl.BlockSpec((1, M, W), per_b), pl.BlockSpec((1, M, W), per_b),
                              pl.BlockSpec((CONV_W, W), const), pl.BlockSpec((1, W), const),
                              pl.BlockSpec((1, W), const), pl.BlockSpec((1, W), const)],
        out_specs=[blk] * 3,
        out_shape=[jax.ShapeDtypeStruct((B, S, W), BF16)] * 3,
        scratch_shapes=[pltpu.VMEM((tm + CONV_HALO, W), F32)],
        compiler_params=_cparams(("parallel", "parallel")),
    )(*os_, *ls_, glu, glu, qm, mk, mv, wdw, bdw, lng, lnb)


SROWS = 8


def _sample_mix_kernel(q0_ref, q1_ref, q2_ref, kn_ref, vn_ref, glu_ref, qm_ref, wk_ref, wv_ref,
                       mk_ref, mv_ref, st_ref, wdw_ref, bdw_ref, lng_ref, lnb_ref,
                       a_ref, b_ref, m_ref, nst_ref):
    n_buf = wk_ref.shape[1] // HEADS_PER_GROUP
    n_mem = mk_ref.shape[1] // N_HEADS_M
    head = lambda ref, j, n: ref[0, pl.ds(j, n, stride=HEADS_PER_GROUP), :].astype(BF16)
    row = lax.broadcasted_iota(jnp.int32, (SROWS, n_buf), 0)
    dist = n_buf - lax.broadcasted_iota(jnp.int32, (SROWS, n_buf), 1)
    (w0, d0), (w1, d1), (w2, d2) = DIL_GROUPS
    dmask = jnp.where(row == 0, d0 - 1, jnp.where(row == 1, d1 - 1, d2 - 1))
    win = jnp.where(row == 0, w0, jnp.where(row == 1, w1, w2))
    valid = jnp.logical_and((dist & dmask) == 0, dist <= win)
    for j in range(HEADS_PER_GROUP):
        sl = slice(j * HEAD_DIM, (j + 1) * HEAD_DIM)
        q8 = jnp.where(row[:, :HEAD_DIM] == 0, q0_ref[0, :, sl],
                       jnp.where(row[:, :HEAD_DIM] == 1, q1_ref[0, :, sl], q2_ref[0, :, sl]))
        s = jnp.where(valid, _dot_nt(q8.astype(BF16), head(wk_ref, j, n_buf)) * SCALE, NEG)
        s_new = jnp.sum(q8 * kn_ref[0, :, sl], axis=-1, keepdims=True) * SCALE
        mx = jnp.maximum(jnp.max(s, axis=-1, keepdims=True), s_new)
        p = jnp.exp(s - mx)
        p_new = jnp.exp(s_new - mx)
        den = jnp.sum(p, axis=-1, keepdims=True) + p_new
        o = (_dot(p.astype(BF16), head(wv_ref, j, n_buf)) + p_new * vn_ref[0, :, sl]) / den
        lse = mx + jnp.log(den)
        lm = jnp.maximum(jnp.maximum(lse[0:1], lse[1:2]), lse[2:3])
        g0, g1, g2 = jnp.exp(lse[0:1] - lm), jnp.exp(lse[1:2] - lm), jnp.exp(lse[2:3] - lm)
        a_ref[0, :, sl] = (g0 * o[0:1] + g1 * o[1:2] + g2 * o[2:3]) / (g0 + g1 + g2)
        qm8 = jnp.broadcast_to(qm_ref[0, :, sl], (SROWS, HEAD_DIM)).astype(BF16)
        mo = _mem_attend(qm8, head(mk_ref, j, n_mem), head(mv_ref, j, n_mem))
        m_ref[0, :, sl] = mo[0:1]
    glu = glu_ref[0]
    y = jnp.sum(st_ref[0] * wdw_ref[0:CONV_W - 1, :], axis=0, keepdims=True) + glu * wdw_ref[CONV_W - 1:CONV_W, :]
    b_ref[0] = _conv_tail(y, bdw_ref[...], lng_ref[...], lnb_ref[...])
    nst_ref[0, 0:CONV_W - 2, :] = st_ref[0, 1:CONV_W - 1, :]
    nst_ref[0, CONV_W - 2:CONV_W - 1, :] = glu


def _sample_mix(qs, kn, vn, glu, qm, wk, wv, mk, mv, st, wdw, bdw, lng, lnb):
    N, W = st.shape[0], st.shape[2]
    one = pl.BlockSpec((1, 1, W), lambda b: (b, 0, 0))
    per = lambda r: pl.BlockSpec((1, r, W), lambda b: (b, 0, 0))
    cache = lambda t: pl.BlockSpec((1,) + t.shape[1:], lambda b: (b, 0, 0))
    const = lambda b: (0, 0)
    return pl.pallas_call(
        _sample_mix_kernel,
        grid=(N,),
        in_specs=[one] * 7 + [cache(wk), cache(wv), cache(mk), cache(mv), per(CONV_W - 1),
                              pl.BlockSpec((CONV_W, W), const), pl.BlockSpec((1, W), const),
                              pl.BlockSpec((1, W), const), pl.BlockSpec((1, W), const)],
        out_specs=[one, one, one, per(CONV_W - 1)],
        out_shape=[jax.ShapeDtypeStruct((N, 1, W), F32)] * 3 + [jax.ShapeDtypeStruct((N, CONV_W - 1, W), F32)],
        compiler_params=_cparams(("parallel",)),
    )(*qs, kn, vn, glu, qm, wk, wv, mk, mv, st, wdw, bdw, lng, lnb)


def _back_kernel(x_ref, a_ref, b_ref, m_ref, g1_ref, wg_ref, bg_ref, wa_ref, wb_ref, wm_ref, wo_ref, g2_ref,
                 x1_ref, h2_ref):
    x = x_ref[...]
    hb = _rms(x, g1_ref[...]).astype(BF16)
    merged = None
    for i, (t_ref, w_ref) in enumerate(((a_ref, wa_ref), (b_ref, wb_ref), (m_ref, wm_ref))):
        sl = slice(i * D_MODEL, (i + 1) * D_MODEL)
        gate = _sigmoid(_dot(hb, wg_ref[:, sl]) + bg_ref[:, sl])
        term = gate * _dot(t_ref[...], w_ref[...])
        merged = term if merged is None else merged + term
    x1 = x + _dot(merged.astype(BF16), wo_ref[...])
    x1_ref[...] = x1
    h2_ref[...] = _pack_rows(_rms(x1, g2_ref[...]).astype(BF16))


def _back(x, a, b, m, g1, wg, bg, wa, wb, wm, wo, g2, tm):
    N, D = x.shape
    W = a.shape[1]
    row = lambda i: (i, 0)
    const = lambda i: (0, 0)
    return pl.pallas_call(
        _back_kernel,
        grid=(N // tm,),
        in_specs=[pl.BlockSpec((tm, D), row)] + [pl.BlockSpec((tm, W), row)] * 3 +
                 [pl.BlockSpec((1, D), const), pl.BlockSpec((D, 3 * D), const), pl.BlockSpec((1, 3 * D), const),
                  pl.BlockSpec((W, D), const), pl.BlockSpec((W, D), const), pl.BlockSpec((W, D), const),
                  pl.BlockSpec((D, D), const), pl.BlockSpec((1, D), const)],
        out_specs=[pl.BlockSpec((tm, D), row), pl.BlockSpec((tm // 2, D), row)],
        out_shape=[jax.ShapeDtypeStruct((N, D), F32), jax.ShapeDtypeStruct((N // 2, D), jnp.int32)],
        compiler_params=_cparams(("parallel",)),
    )(x, a, b, m, g1, wg, bg, wa, wb, wm, wo, g2)


PEER_CAND_ROWS = 56


def _top_sorted(s, k):
    out = []
    for _ in range(k):
        m = jnp.max(s, axis=0, keepdims=True)
        out.append(m)
        s = jnp.where(s == m, NEG, s)
    return out


def _prefix_count(test, t):
    c8 = test(t[7])
    c4 = test(jnp.where(c8, t[11], t[3]))
    c2 = test(jnp.where(c8, jnp.where(c4, t[13], t[9]), jnp.where(c4, t[5], t[1])))
    c1 = test(jnp.where(c8, jnp.where(c4, jnp.where(c2, t[14], t[12]), jnp.where(c2, t[10], t[8])),
                        jnp.where(c4, jnp.where(c2, t[6], t[4]), jnp.where(c2, t[2], t[0]))))
    n = (jnp.where(c8, 8.0, 0.0) + jnp.where(c4, 4.0, 0.0)) + (jnp.where(c2, 2.0, 0.0) + jnp.where(c1, 1.0, 0.0))
    return jnp.where(test(t[15]), 16.0, n)


def _peer_route_kernel(h_ref, wq_ref, sk_ref, r2_ref, e2_ref, qd_ref, e1d_ref, cand_ref):
    hb = _unpack_rows(h_ref[...])
    pairs = [(p, q) for p in range(PEER_TOPK) for q in range(PEER_TOPK) if (p + 1) * (q + 1) <= PEER_TOPK]
    cand_ref[...] = jnp.full(cand_ref.shape, NEG, F32)
    for h in range(PEER_HEADS):
        r0 = h * 2 * N_KEYS
        q1 = _dot_nt(wq_ref[r0:r0 + N_KEYS, :], hb).astype(BF16)
        q2 = _dot_nt(wq_ref[r0 + N_KEYS:r0 + 2 * N_KEYS, :], hb).astype(BF16)
        s1 = _dot(sk_ref[0], q1)
        s2 = _dot(sk_ref[1], q2)
        t1 = _top_sorted(s1, PEER_TOPK)
        t2 = _top_sorted(s2, PEER_TOPK)
        for r, (p, q) in enumerate(pairs):
            cand_ref[r:r + 1, :] = t1[p] + t2[q]
        top = _top_sorted(cand_ref[...], PEER_TOPK)
        z = jnp.exp(top[0] - top[0])
        for k in range(1, PEER_TOPK):
            z = z + jnp.exp(top[k] - top[0])
        tau = top[PEER_TOPK - 1]
        r2_ref[h] = _pack_rows(_prefix_count(lambda v: v > s2, t2).astype(BF16))
        qd_ref[h] = _prefix_count(lambda v: s1 + v >= tau, t2)
        e1d_ref[h] = jnp.exp(s1 - t1[0])
        e2_ref[h] = _pack_rows((jnp.exp(s2 - t2[0]) * (0.5 / z)).astype(BF16))


def _peer_route(h2, wq_t, sk, t):
    N, D = 2 * h2.shape[0], h2.shape[1]
    blk = lambda r: pl.BlockSpec((PEER_HEADS, r, t), lambda i: (0, 0, i))
    sds = lambda r: jax.ShapeDtypeStruct((PEER_HEADS, r, N), jnp.int32 if r < N_KEYS else F32)
    return pl.pallas_call(
        _peer_route_kernel,
        grid=(N // t,),
        in_specs=[pl.BlockSpec((t // 2, D), lambda i: (i, 0)),
                  pl.BlockSpec(wq_t.shape, lambda i: (0, 0)),
                  pl.BlockSpec(sk.shape, lambda i: (0, 0, 0))],
        out_specs=[blk(N_KEYS // 2), blk(N_KEYS // 2), blk(N_KEYS), blk(N_KEYS)],
        out_shape=[sds(N_KEYS // 2), sds(N_KEYS // 2), sds(N_KEYS), sds(N_KEYS)],
        scratch_shapes=[pltpu.VMEM((PEER_CAND_ROWS, t), F32)],
        compiler_params=_cparams(("parallel",)),
    )(h2, wq_t, sk)


PEER_CI = 16
PEER_SUB = 4
PEER_LANES = 256
PEER_W_LANES = 128
PEER_OUT_SPLIT = 2
GELU_C = 0.7978845608028654
GELU_A = 0.044715


def _peer_dense_kernel(h_ref, x1_ref, u_ref, vt_ref, r2_ref, e2_ref, qd_ref, e1d_ref, y_ref, acc_ref):
    c = pl.program_id(1)
    t = x1_ref.shape[0]
    lanes = min(PEER_LANES, t)
    sub = PEER_SUB * N_KEYS
    n_sub = PEER_CI // PEER_SUB

    @pl.when(c == 0)
    def _():
        acc_ref[...] = jnp.zeros_like(acc_ref)

    wl = min(PEER_W_LANES, lanes)

    def row_bf16(row, w0):
        return jnp.broadcast_to(row[:, w0:w0 + wl], (N_KEYS, wl)).astype(BF16)

    zero = jnp.zeros((N_KEYS, wl), BF16)
    subs = [(l0, q) for l0 in range(0, t, lanes) for q in range(n_sub)]

    def first_matmul(k):
        l0, q = subs[k]
        u = _unpack_rows(u_ref[q * sub // 2:(q + 1) * sub // 2, :])
        return _dot_nt(u, _unpack_rows(h_ref[l0 // 2:(l0 + lanes) // 2, :]))

    def gated(k, at):
        l0, q = subs[k]
        rows = []
        for ii in range(PEER_SUB):
            i = c * PEER_CI + q * PEER_SUB + ii
            counts = [qd_ref[h, pl.ds(i, 1), l0:l0 + lanes] for h in range(PEER_HEADS)]
            e1s = [e1d_ref[h, pl.ds(i, 1), l0:l0 + lanes] for h in range(PEER_HEADS)]
            cols = []
            for w0 in range(0, lanes, wl):
                ls = slice(l0 + w0, l0 + w0 + wl)
                w = None
                for h in range(PEER_HEADS):
                    sel = _unpack_rows(r2_ref[h, :, ls]) < row_bf16(counts[h], w0)
                    term = jnp.where(sel, _unpack_rows(e2_ref[h, :, ls]), zero) * row_bf16(e1s[h], w0)
                    w = term if w is None else w + term
                a = at[ii * N_KEYS:(ii + 1) * N_KEYS, w0:w0 + wl].astype(BF16)
                inner = a * ((a * a) * (GELU_C * GELU_A) + GELU_C)
                cols.append(w * (a * jnp.tanh(inner) + a))
            rows.append(cols[0] if len(cols) == 1 else jnp.concatenate(cols, axis=1))
        return jnp.concatenate(rows, axis=0)

    parts = {}

    d_model = acc_ref.shape[0]
    out_rows = d_model // PEER_OUT_SPLIT

    def second_matmul(k, hk):
        l0, q = subs[k]
        for o0 in range(0, d_model, out_rows):
            d = _dot(_unpack_rows(vt_ref[o0 // 2:(o0 + out_rows) // 2, q * sub:(q + 1) * sub]), hk)
            parts[l0, o0] = d if q == 0 else parts[l0, o0] + d
            if q == n_sub - 1:
                acc_ref[o0:o0 + out_rows, l0:l0 + lanes] += parts[l0, o0]

    at_next = first_matmul(0)
    h_prev = None
    for k in range(len(subs)):
        at = at_next
        if k + 1 < len(subs):
            at_next = first_matmul(k + 1)
        if k > 0:
            second_matmul(k - 1, h_prev)
        h_prev = gated(k, at)
    second_matmul(len(subs) - 1, h_prev)

    @pl.when(c == pl.num_programs(1) - 1)
    def _():
        y_ref[...] = x1_ref[...] + acc_ref[...].T


def _peer_dense(h2, x1, u_p, vt_p, r2, e2, qd, e1d, t):
    N, D = x1.shape
    E = 2 * u_p.shape[0]
    ce = PEER_CI * N_KEYS
    tok = lambda i, c: (i, 0)
    rt = lambda r: pl.BlockSpec((PEER_HEADS, r, t), lambda i, c: (0, 0, i))
    return pl.pallas_call(
        _peer_dense_kernel,
        grid=(N // t, E // ce),
        in_specs=[pl.BlockSpec((t // 2, D), tok), pl.BlockSpec((t, D), tok),
                  pl.BlockSpec((ce // 2, D), lambda i, c: (c, 0)),
                  pl.BlockSpec((D // 2, ce), lambda i, c: (0, c)),
                  rt(N_KEYS // 2), rt(N_KEYS // 2), rt(N_KEYS), rt(N_KEYS)],
        out_specs=pl.BlockSpec((t, D), tok),
        out_shape=jax.ShapeDtypeStruct((N, D), F32),
        scratch_shapes=[pltpu.VMEM((D, t), F32)],
        compiler_params=_cparams(("parallel", "arbitrary")),
    )(h2, x1, u_p, vt_p, r2, e2, qd, e1d)


def _peer(h2, x1, wq_t, sk, u_p, vt_p, t):
    N = x1.shape[0]
    pad = -N % t
    if pad:
        h2 = jnp.pad(h2, ((0, pad // 2), (0, 0)))
        x1 = jnp.pad(x1, ((0, pad), (0, 0)))
    routed = _peer_route(h2, wq_t, sk, min(t, PEER_LANES))
    return _peer_dense(h2, x1, u_p, vt_p, *routed, t)[:N]


def kernel(x_prompt, x_sample, mem_prompt, cache_win_k, cache_win_v, cache_mem_k, cache_mem_v, state_conv, g_norm1, w_in, b_gate, qn_a, kn_a, qn_m, kn_m, g_mem, w_mem_kv, w_dw, b_dw, ln_g, ln_b, w_a_proj, w_b_proj, w_m_proj, w_o, g_norm2, w_pq, sub_keys, u_tab, v_tab):
    B, S, D = x_prompt.shape
    NS = x_sample.shape[0]
    assert cache_win_k.shape[0] == 1 and x_sample.shape[1] == 1
    n_win = min(DIL_GROUPS[-1][0], S)
    tm = min(512, S)
    tp = min(512, B * S)

    w_main = w_in[0, :, :IN_MAIN_W].astype(BF16)
    w_gate = w_in[0, :, IN_MAIN_W:].astype(BF16)
    row = lambda v: v.reshape(1, -1)
    g1, g2, bg = row(g_norm1[0]), row(g_norm2[0]), row(b_gate[0])
    qna, kna, qnm, knm = row(qn_a[0]), row(kn_a[0]), row(qn_m[0]), row(kn_m[0])
    bdw, lng, lnb = row(b_dw[0]), row(ln_g[0]), row(ln_b[0])
    wdw = w_dw[0]
    wa, wb, wm, wo = (w[0].astype(BF16) for w in (w_a_proj, w_b_proj, w_m_proj, w_o))
    wq_t = w_pq[0].T.astype(BF16)
    sk = sub_keys[0].astype(BF16)
    u_b = _pack_table(u_tab[0], transpose=False)
    vt_b = _pack_table(v_tab[0], transpose=True)

    q0, q1, q2, kaf, vaf, glu, qm = _front(x_prompt, jnp.arange(S), g1, w_main, qna, kna, qnm, tm)
    outs = [_band_group(q, kaf, vaf, dil, max(BAND_STEP_ROWS, dil * BAND_BLK))
            for q, (_, dil) in zip((q0, q1, q2), DIL_GROUPS)]
    mk, mv = _mem_kv(mem_prompt, row(g_mem[0]), w_mem_kv[0].astype(BF16), knm)
    a, b, m = _prompt_mix([o for o, _ in outs], [l for _, l in outs], glu, qm, mk, mv, wdw, bdw, lng, lnb, tm)
    flat = lambda t: t.reshape(B * S, -1)
    x1, h2 = _back(flat(x_prompt), flat(a), flat(b), flat(m), g1, w_gate, bg, wa, wb, wm, wo, g2, tm)
    y_prompt = _peer(h2, x1, wq_t, sk, u_b, vt_b, tp).reshape(B, S, D)

    xs = x_sample.reshape(1, NS, D)
    pos_s = jnp.full((NS,), PAST_LEN, jnp.int32)
    sq0, sq1, sq2, skaf, svaf, sglu, sqm = _front(xs, pos_s, g1, w_main, qna, kna, qnm, NS)
    tok = lambda t: t.reshape(NS, 1, -1).astype(F32)
    cw = lambda t: t.reshape(NS, -1, HEAD_DIM)
    sa, sb, sm, new_conv_s = _sample_mix(
        (tok(sq0), tok(sq1), tok(sq2)), tok(skaf), tok(svaf), tok(sglu), tok(sqm),
        cw(cache_win_k), cw(cache_win_v), cw(cache_mem_k), cw(cache_mem_v), state_conv[0],
        wdw, bdw, lng, lnb)
    mixed = lambda t: t.reshape(NS, -1).astype(BF16)
    sx1, sh2 = _back(x_sample.reshape(NS, D), mixed(sa), mixed(sb), mixed(sm),
                     g1, w_gate, bg, wa, wb, wm, wo, g2, NS)
    y_sample = _peer(sh2, sx1, wq_t, sk, u_b, vt_b, 128).reshape(NS, 1, D)

    heads = lambda t, n: t.reshape(1, t.shape[0], n, HEADS_PER_GROUP, HEAD_DIM)
    return (y_prompt, y_sample,
            heads(kaf[:, S - n_win:], n_win), heads(vaf[:, S - n_win:], n_win),
            heads(mk, N_MEM), heads(mv, N_MEM),
            glu[None, :, S - (CONV_W - 1):],
            heads(skaf.reshape(NS, 1, -1), 1), heads(svaf.reshape(NS, 1, -1), 1),
            new_conv_s[None])
```
